```python
import jax, jax.numpy as jnp
from jax import lax
import numpy as np

D_MODEL = 1024
BATCH = 8
SEQ = 8192
DEPTH = 1
DEC_BATCH = 8
DEC_SEQ = 4096
PAST_LEN = 128

N_META = 16
N_HEADS = 16
N_KV_HEADS = 4
HEAD_DIM = D_MODEL // N_HEADS
Q_W = N_HEADS * HEAD_DIM
KV_W = N_KV_HEADS * HEAD_DIM
WINDOW = 128
BLOCK = 128
ROPE_THETA = 10000.0
CONV_WIDTH = D_MODEL
CONV_K = 3
N_EXPERTS = 16
CAPACITY_FACTOR = 2
D_EXPERT = 2 * D_MODEL
EPS = 1e-6
PROJ_SIZES = (Q_W, KV_W, KV_W, CONV_WIDTH, CONV_WIDTH, CONV_WIDTH, D_MODEL, D_MODEL)
PROJ_WIDTH = sum(PROJ_SIZES)
PROJ_SPLITS = [int(s) for s in np.cumsum(PROJ_SIZES)[:-1]]

kernel_name = "hybrid_conv_swa_ec_moe_encoder"


def _rmsnorm(x, g):
    xf = x.astype(jnp.float32)
    xf = xf * lax.rsqrt(jnp.mean(xf * xf, axis=-1, keepdims=True) + EPS)
    return (xf * g.astype(jnp.float32)).astype(x.dtype)


def _rope_tables(length):
    inv = ROPE_THETA ** (-jnp.arange(0, HEAD_DIM, 2, dtype=jnp.float32) / HEAD_DIM)
    ang = jnp.arange(length, dtype=jnp.float32)[:, None] * inv[None, :]
    return jnp.cos(ang)[:, None, :], jnp.sin(ang)[:, None, :]


def _rope(x, cos, sin):
    xf = x.astype(jnp.float32)
    x1, x2 = jnp.split(xf, 2, axis=-1)
    return jnp.concatenate([x1 * cos - x2 * sin, x2 * cos + x1 * sin], axis=-1).astype(x.dtype)


def _short_conv(u, w):
    up = jnp.pad(u, ((0, 0), (1, 1), (0, 0)))
    return w[0] * up[:, :-2] + w[1] * up[:, 1:-1] + w[2] * up[:, 2:]


def _window_attention(q, k, v, sink):
    B, L = q.shape[0], q.shape[1]
    G = N_HEADS // N_KV_HEADS
    nb = -(-L // BLOCK)
    Lp = nb * BLOCK
    qb = jnp.pad(q, ((0, 0), (0, Lp - L), (0, 0), (0, 0)))
    qb = qb.reshape(B, nb, BLOCK, N_KV_HEADS, G, HEAD_DIM).transpose(1, 0, 2, 3, 4, 5)
    pad_kv = ((0, 0), (BLOCK, Lp - L + BLOCK), (0, 0), (0, 0))
    kp = jnp.pad(k, pad_kv)
    vp = jnp.pad(v, pad_kv)
    sink_f = sink.astype(jnp.float32).reshape(1, N_KV_HEADS, G, 1, 1)
    scale = HEAD_DIM ** -0.5

    def block(args):
        i, qi = args
        start = i * BLOCK
        ks = lax.dynamic_slice_in_dim(kp, start, 3 * BLOCK, axis=1)
        vs = lax.dynamic_slice_in_dim(vp, start, 3 * BLOCK, axis=1)
        qpos = start + jnp.arange(BLOCK)
        kpos = start - BLOCK + jnp.arange(3 * BLOCK)
        mask = ((jnp.abs(qpos[:, None] - kpos[None, :]) <= WINDOW)
                & (kpos >= 0)[None, :] & (kpos < L)[None, :])
        s = jnp.einsum('bqkgd,bskd->bkgqs', qi, ks).astype(jnp.float32) * scale
        s = jnp.where(mask, s, -1e30)
        m = jnp.maximum(s.max(axis=-1, keepdims=True), sink_f)
        p = jnp.exp(s - m)
        denom = p.sum(axis=-1, keepdims=True) + jnp.exp(sink_f - m)
        return jnp.einsum('bkgqs,bskd->bqkgd', (p / denom).astype(vs.dtype), vs)

    out = lax.map(block, (jnp.arange(nb), qb))
    return out.transpose(1, 0, 2, 3, 4, 5).reshape(B, Lp, Q_W)[:, :L]


def _expert_choice_ffn(h, w_router, w_gate, w_up, w_down):
    B, L, D = h.shape
    n_tok = B * L
    hf = h.reshape(n_tok, D)
    cap = max(1, CAPACITY_FACTOR * n_tok // N_EXPERTS)
    affinity = jax.nn.softmax((hf @ w_router).astype(jnp.float32), axis=-1)
    gate_vals, tok_idx = lax.top_k(affinity.T, cap)
    xs = hf[tok_idx]

    def expert(args):
        xe, wg, wu, wd, ge = args
        he = jax.nn.silu(xe @ wg) * (xe @ wu)
        return (he @ wd) * ge[:, None].astype(xe.dtype)

    ye = lax.map(expert, (xs, w_gate, w_up, w_down, gate_vals))
    out = jnp.zeros_like(hf).at[tok_idx.reshape(-1)].add(ye.reshape(-1, D))
    return out.reshape(B, L, D)


def _encoder(x, meta_tokens, g_mix, w_in, g_q, g_k, sink_logits, w_conv, w_attn_out,
             w_conv_out, w_out, g_ffn, w_router, w_expert_gate, w_expert_up, w_expert_down):
    B = x.shape[0]
    meta = jnp.broadcast_to(meta_tokens.astype(x.dtype)[None], (B, N_META, D_MODEL))
    x = jnp.concatenate([meta, x], axis=1)
    L = x.shape[1]
    cos, sin = _rope_tables(L)
    for l in range(DEPTH):
        h = _rmsnorm(x, g_mix[l])
        proj = h @ w_in[l]
        q, k, v, cb, cc, cu, ga, gc = jnp.split(proj, PROJ_SPLITS, axis=-1)
        q = _rope(_rmsnorm(q.reshape(B, L, N_HEADS, HEAD_DIM), g_q[l]), cos, sin)
        k = _rope(_rmsnorm(k.reshape(B, L, N_KV_HEADS, HEAD_DIM), g_k[l]), cos, sin)
        v = v.reshape(B, L, N_KV_HEADS, HEAD_DIM)
        attn = _window_attention(q, k, v, sink_logits[l]) @ w_attn_out[l]
        conv = (cb * _short_conv(cc * cu, w_conv[l])) @ w_conv_out[l]
        mix = (jax.nn.sigmoid(ga) * attn + jax.nn.sigmoid(gc) * conv) @ w_out[l]
        x = x + mix
        x = x + _expert_choice_ffn(_rmsnorm(x, g_ffn[l]), w_router[l], w_expert_gate[l],
                                   w_expert_up[l], w_expert_down[l])
    return x[:, N_META:]


def setup_inputs(seed: int = 0) -> dict:
    key = jax.random.key(seed)
    ks = jax.random.split(key, 20)

    def nrm(k, shape, scale):
        return jax.random.normal(k, shape, jnp.float32) * scale

    return {
        "x_prompt": nrm(ks[0], (BATCH, SEQ, D_MODEL), 1.0),
        "x_sample": nrm(ks[1], (DEC_BATCH, DEC_SEQ, D_MODEL), 1.0),
        "meta_tokens": nrm(ks[2], (N_META, D_MODEL), 1.0),
        "g_mix": 1.0 + nrm(ks[3], (DEPTH, D_MODEL), 0.02),
        "w_in": nrm(ks[4], (DEPTH, D_MODEL, PROJ_WIDTH), D_MODEL ** -0.5),
        "g_q": 1.0 + nrm(ks[5], (DEPTH, HEAD_DIM), 0.02),
        "g_k": 1.0 + nrm(ks[6], (DEPTH, HEAD_DIM), 0.02),
        "sink_logits": nrm(ks[7], (DEPTH, N_HEADS), 0.5),
        "w_conv": nrm(ks[8], (DEPTH, CONV_K, CONV_WIDTH), CONV_K ** -0.5),
        "w_attn_out": nrm(ks[9], (DEPTH, Q_W, D_MODEL), Q_W ** -0.5),
        "w_conv_out": nrm(ks[10], (DEPTH, CONV_WIDTH, D_MODEL), CONV_WIDTH ** -0.5),
        "w_out": nrm(ks[11], (DEPTH, D_MODEL, D_MODEL), D_MODEL ** -0.5),
        "g_ffn": 1.0 + nrm(ks[12], (DEPTH, D_MODEL), 0.02),
        "w_router": nrm(ks[13], (DEPTH, D_MODEL, N_EXPERTS), D_MODEL ** -0.5),
        "w_expert_gate": nrm(ks[14], (DEPTH, N_EXPERTS, D_MODEL, D_EXPERT), D_MODEL ** -0.5),
        "w_expert_up": nrm(ks[15], (DEPTH, N_EXPERTS, D_MODEL, D_EXPERT), D_MODEL ** -0.5),
        "w_expert_down": nrm(ks[16], (DEPTH, N_EXPERTS, D_EXPERT, D_MODEL), D_EXPERT ** -0.5),
    }


def reference(x_prompt, x_sample, meta_tokens, g_mix, w_in, g_q, g_k, sink_logits, w_conv,
              w_attn_out, w_conv_out, w_out, g_ffn, w_router, w_expert_gate, w_expert_up,
              w_expert_down):
    y_prompt = _encoder(x_prompt, meta_tokens, g_mix, w_in, g_q, g_k, sink_logits, w_conv,
                        w_attn_out, w_conv_out, w_out, g_ffn, w_router, w_expert_gate,
                        w_expert_up, w_expert_down)
    y_sample = _encoder(x_sample, meta_tokens, g_mix, w_in, g_q, g_k, sink_logits, w_conv,
                        w_attn_out, w_conv_out, w_out, g_ffn, w_router, w_expert_gate,
                        w_expert_up, w_expert_down)
    return (y_prompt, y_sample)
```

```python
import functools

import numpy as np
import jax
import jax.numpy as jnp
from jax import lax
from jax.experimental import pallas as pl
from jax.experimental.pallas import tpu as pltpu

N_META = 16
N_HEADS = 16
N_KV_HEADS = 4
HEAD_DIM = 64
N_EXPERTS = 16
CAPACITY_FACTOR = 2
ROPE_THETA = 10000.0
EPS = 1e-6
NEG = -1e30

LANES = 128
BLK = 128
META_ROW0 = BLK - N_META
TM = 512
HALO = 16
VMEM_LIMIT = 56 * 1024 * 1024

F32 = jnp.float32
BF16 = jnp.bfloat16


def _dot(a, b):
    return jnp.dot(a, b, preferred_element_type=F32)


def _dot_nt(a, b):
    return lax.dot_general(a, b, (((1,), (1,)), ((), ())), preferred_element_type=F32)


def _cparams(n_axes):
    return pltpu.CompilerParams(dimension_semantics=("arbitrary",) * n_axes,
                                vmem_limit_bytes=VMEM_LIMIT)


def _proj_kernel(x_ref, meta_ref, cos_ref, sin_ref, gmix_ref, gq_ref, gk_ref, bd_ref, w_ref,
                 q_ref, k_ref, v_ref, cb_ref, s_ref, ga_ref, gc_ref, *, nreal, d, qw, kvw, cw):
    j = pl.program_id(1)
    lane = lax.broadcasted_iota(jnp.int32, (1, LANES), 1)
    first_half = jnp.bitwise_and(lane, HEAD_DIM - 1) < (HEAD_DIM // 2)

    def norm_rope(t, g, cos, sin):
        ss = _dot((t * t).astype(BF16), bd_ref[...])
        t = t * lax.rsqrt(ss * (1.0 / HEAD_DIM) + EPS) * g
        partner = jnp.where(first_half, pltpu.roll(t, LANES - HEAD_DIM // 2, 1),
                            pltpu.roll(t, HEAD_DIM // 2, 1))
        return t * cos + partner * sin

    def body(x, cos, sin, rows):
        ms = jnp.mean(x * x, axis=-1, keepdims=True)
        h = (x * lax.rsqrt(ms + EPS) * gmix_ref[...]).astype(BF16)
        off = 0
        for c in range(qw // LANES):
            t = _dot(h, w_ref[:, off:off + LANES])
            t = norm_rope(t, gq_ref[...], cos, sin) * (HEAD_DIM ** -0.5)
            q_ref[0, :rows, c * LANES:(c + 1) * LANES] = t.astype(BF16)
            off += LANES
        for c in range(kvw // LANES):
            t = _dot(h, w_ref[:, off:off + LANES])
            k_ref[0, :rows, c * LANES:(c + 1) * LANES] = norm_rope(t, gk_ref[...], cos, sin).astype(BF16)
            off += LANES
        v_ref[0, :rows, :] = _dot(h, w_ref[:, off:off + kvw]).astype(BF16)
        off += kvw
        cb_ref[0, :rows, :] = _dot(h, w_ref[:, off:off + cw]).astype(BF16)
        off += cw
        cc = _dot(h, w_ref[:, off:off + cw])
        off += cw
        cu = _dot(h, w_ref[:, off:off + cw])
        off += cw
        s_ref[0, :rows, :] = (cc * cu).astype(BF16)
        ga_ref[0, :rows, :] = jax.nn.sigmoid(_dot(h, w_ref[:, off:off + d])).astype(BF16)
        off += d
        gc_ref[0, :rows, :] = jax.nn.sigmoid(_dot(h, w_ref[:, off:off + d])).astype(BF16)

    @pl.when(j < nreal)
    def _():
        body(x_ref[0], cos_ref[...], sin_ref[...], TM)

    @pl.when(j == nreal)
    def _():
        body(meta_ref[...], cos_ref[:BLK, :], sin_ref[:BLK, :], BLK)


def _stage_a(x, meta_blk, cos2, sin2, gmix, gq2, gk2, bd, w_in_b):
    b, s, d = x.shape
    li = s + BLK
    nreal = s // TM
    qw, kvw, cw = N_HEADS * HEAD_DIM, N_KV_HEADS * HEAD_DIM, d
    const = lambda bi, j: (0, 0)
    rowblk = lambda bi, j: (bi, j, 0)
    out_w = (qw, kvw, kvw, cw, cw, d, d)
    return pl.pallas_call(
        functools.partial(_proj_kernel, nreal=nreal, d=d, qw=qw, kvw=kvw, cw=cw),
        grid=(b, nreal + 1),
        in_specs=[
            pl.BlockSpec((1, TM, d), lambda bi, j: (bi, jnp.minimum(j, nreal - 1), 0)),
            pl.BlockSpec((BLK, d), const),
            pl.BlockSpec((TM, LANES), lambda bi, j: (j, 0)),
            pl.BlockSpec((TM, LANES), lambda bi, j: (j, 0)),
            pl.BlockSpec((1, d), const),
            pl.BlockSpec((1, LANES), const),
            pl.BlockSpec((1, LANES), const),
            pl.BlockSpec((LANES, LANES), const),
            pl.BlockSpec(w_in_b.shape, const, pipeline_mode=pl.Buffered(1)),
        ],
        out_specs=[pl.BlockSpec((1, TM, w), rowblk) for w in out_w],
        out_shape=[jax.ShapeDtypeStruct((b, li, w), BF16) for w in out_w],
        compiler_params=_cparams(2),
        name="proj_stage",
    )(x, meta_blk, cos2, sin2, gmix, gq2, gk2, bd, w_in_b)


def _attn_kernel(q_ref, kp_ref, kc_ref, kn_ref, vp_ref, vc_ref, vn_ref, sink_ref, o_ref, *, nblk):
    sb = pl.program_id(1)
    qi = lax.broadcasted_iota(jnp.int32, (BLK, 3 * BLK), 0)
    cj = lax.broadcasted_iota(jnp.int32, (BLK, 3 * BLK), 1)
    band = (cj >= qi) & (cj <= qi + 2 * BLK)
    seg = jnp.right_shift(cj, 7)
    row = jnp.bitwise_and(cj, BLK - 1)
    lo_prev = jnp.where(sb == 0, META_ROW0, jnp.where(sb == nblk - 1, BLK, 0))
    lo_cur = jnp.where(sb == nblk - 1, META_ROW0, 0)
    lo_next = jnp.where(sb == nblk - 2, BLK, 0)
    lo = jnp.where(seg == 0, lo_prev, jnp.where(seg == 1, lo_cur, lo_next))
    ok = (band & (row >= lo))[None]
    lane = lax.broadcasted_iota(jnp.int32, (1, LANES), 1)
    first = lane < HEAD_DIM
    zero = jnp.zeros((), BF16)
    groups = N_HEADS // N_KV_HEADS
    for m in range(N_KV_HEADS * HEAD_DIM // LANES):
        cols = slice(m * LANES, (m + 1) * LANES)
        kcat = jnp.concatenate([kp_ref[0, :, cols], kc_ref[0, :, cols], kn_ref[0, :, cols]], axis=0)
        vcat = jnp.concatenate([vp_ref[0, :, cols], vc_ref[0, :, cols], vn_ref[0, :, cols]], axis=0)
        pieces = []
        for i in range(groups):
            qc = q_ref[0, :, (groups * m + i) * LANES:(groups * m + i + 1) * LANES]
            pieces += [jnp.where(first, qc, zero), jnp.where(first, zero, qc)]
        lhs = jnp.concatenate(pieces, axis=0)
        nrow = lhs.shape[0]
        s = _dot_nt(lhs, kcat)
        s = jnp.where(ok, s.reshape(nrow // BLK, BLK, 3 * BLK), NEG).reshape(nrow, 3 * BLK)
        sink = sink_ref[m]
        mx = jnp.maximum(jnp.max(s, axis=-1, keepdims=True), sink)
        p = jnp.exp(s - mx)
        den = jnp.sum(p, axis=-1, keepdims=True) + jnp.exp(sink - mx)
        o = _dot(p.astype(BF16), vcat) / den
        for i in range(groups):
            oa = o[(2 * i) * BLK:(2 * i + 1) * BLK]
            ob = o[(2 * i + 1) * BLK:(2 * i + 2) * BLK]
            c = groups * m + i
            o_ref[0, :, c * LANES:(c + 1) * LANES] = jnp.where(first, oa, ob).astype(BF16)


def _stage_b(q, k, v, sink_rows):
    b, li, qw = q.shape
    kvw = k.shape[-1]
    nblk = li // BLK
    prev = lambda bi, sb: (bi, (sb + nblk - 1) % nblk, 0)
    cur = lambda bi, sb: (bi, sb, 0)
    nxt = lambda bi, sb: (bi, (sb + 1) % nblk, 0)
    kv_specs = [pl.BlockSpec((1, BLK, kvw), f) for f in (prev, cur, nxt)]
    return pl.pallas_call(
        functools.partial(_attn_kernel, nblk=nblk),
        grid=(b, nblk),
        in_specs=[pl.BlockSpec((1, BLK, qw), cur)] + kv_specs + kv_specs
        + [pl.BlockSpec(sink_rows.shape, lambda bi, sb: (0, 0, 0))],
        out_specs=pl.BlockSpec((1, BLK, qw), cur),
        out_shape=jax.ShapeDtypeStruct((b, li, qw), BF16),
        compiler_params=_cparams(2),
        name="attn_stage",
    )(q, k, k, k, v, v, v, sink_rows)


def _mix_kernel(x_ref, meta_ref, attn_ref, cb_ref, s_ref, sp_ref, sn_ref, ga_ref, gc_ref,
                wconv_ref, gffn_ref, wa_ref, wc_ref, wo_ref, wr_ref,
                x1_ref, h2_ref, aff_ref, *, nreal):
    j = pl.program_id(1)

    def body(x, rows):
        s = s_ref[0, :rows, :].astype(F32)
        ridx = lax.broadcasted_iota(jnp.int32, (rows, 1), 0)
        s_prev = jnp.where(ridx == 0, sp_ref[0, HALO - 1:HALO, :].astype(F32), pltpu.roll(s, 1, 0))
        s_next = jnp.where(ridx == rows - 1, sn_ref[0, 0:1, :].astype(F32), pltpu.roll(s, rows - 1, 0))
        conv = wconv_ref[0:1, :] * s_prev + wconv_ref[1:2, :] * s + wconv_ref[2:3, :] * s_next
        conv_in = (cb_ref[0, :rows, :].astype(F32) * conv).astype(BF16)
        conv_p = _dot(conv_in, wc_ref[...])
        attn_p = _dot(attn_ref[0, :rows, :], wa_ref[...])
        merged = ga_ref[0, :rows, :].astype(F32) * attn_p + gc_ref[0, :rows, :].astype(F32) * conv_p
        x1 = x + _dot(merged.astype(BF16), wo_ref[...])
        x1_ref[0, :rows, :] = x1
        ms = jnp.mean(x1 * x1, axis=-1, keepdims=True)
        h2 = (x1 * lax.rsqrt(ms + EPS) * gffn_ref[...]).astype(BF16)
        h2_ref[0, :rows, :] = h2
        logits = _dot_nt(wr_ref[...], h2)
        e = jnp.exp(logits - jnp.max(logits, axis=0, keepdims=True))
        aff_ref[0, :, :rows] = e / jnp.sum(e, axis=0, keepdims=True)

    @pl.when(j < nreal)
    def _():
        body(x_ref[0], TM)

    @pl.when(j == nreal)
    def _():
        body(meta_ref[...], BLK)


def _stage_c(x, meta_blk, attn, cb, sprod, ga, gc, wconv, gffn, wa, wc, wo, wr_t):
    b, s, d = x.shape
    li = s + BLK
    nreal = s // TM
    nhalo = li // HALO
    const = lambda bi, j: (0, 0)
    rowblk = lambda bi, j: (bi, j, 0)
    halo_prev = lambda bi, j: (bi, (j * (TM // HALO) + nhalo - 1) % nhalo, 0)
    halo_next = lambda bi, j: (bi, jnp.where(j == nreal, 0, (j + 1) * (TM // HALO)), 0)
    n_exp = wr_t.shape[0]
    wspec = lambda w: pl.BlockSpec(w.shape, const, pipeline_mode=pl.Buffered(1))
    return pl.pallas_call(
        functools.partial(_mix_kernel, nreal=nreal),
        grid=(b, nreal + 1),
        in_specs=[
            pl.BlockSpec((1, TM, d), lambda bi, j: (bi, jnp.minimum(j, nreal - 1), 0)),
            pl.BlockSpec((BLK, d), const),
            pl.BlockSpec((1, TM, d), rowblk),
            pl.BlockSpec((1, TM, d), rowblk),
            pl.BlockSpec((1, TM, d), rowblk),
            pl.BlockSpec((1, HALO, d), halo_prev),
            pl.BlockSpec((1, HALO, d), halo_next),
            pl.BlockSpec((1, TM, d), rowblk),
            pl.BlockSpec((1, TM, d), rowblk),
            pl.BlockSpec(wconv.shape, const),
            pl.BlockSpec((1, d), const),
            wspec(wa), wspec(wc), wspec(wo), wspec(wr_t),
        ],
        out_specs=[
            pl.BlockSpec((1, TM, d), rowblk),
            pl.BlockSpec((1, TM, d), rowblk),
            pl.BlockSpec((1, n_exp, TM), lambda bi, j: (bi, 0, j)),
        ],
        out_shape=[
            jax.ShapeDtypeStruct((b, li, d), F32),
            jax.ShapeDtypeStruct((b, li, d), BF16),
            jax.ShapeDtypeStruct((b, n_exp, li), F32),
        ],
        compiler_params=_cparams(2),
        name="mix_stage",
    )(x, meta_blk, attn, cb, sprod, sprod, sprod, ga, gc, wconv, gffn, wa, wc, wo, wr_t)


def _expert_kernel(xs_ref, gate_ref, wg_ref, wu_ref, wd_ref, y_ref):
    x = xs_ref[0]
    g = _dot(x, wg_ref[0])
    u = _dot(x, wu_ref[0])
    he = (g * jax.nn.sigmoid(g) * u).astype(BF16)
    y_ref[0] = _dot(he, wd_ref[0]) * gate_ref[0]


def _stage_d(xs, gates, wg, wu, wd):
    e, cap_p, d = xs.shape
    de = wg.shape[-1]
    tile = lambda ei, t: (ei, t, 0)
    wmap = lambda ei, t: (ei, 0, 0)
    return pl.pallas_call(
        _expert_kernel,
        grid=(e, cap_p // TM),
        in_specs=[
            pl.BlockSpec((1, TM, d), tile),
            pl.BlockSpec((1, TM, 1), tile),
            pl.BlockSpec((1, d, de), wmap),
            pl.BlockSpec((1, d, de), wmap),
            pl.BlockSpec((1, de, d), wmap),
        ],
        out_specs=pl.BlockSpec((1, TM, d), tile),
        out_shape=jax.ShapeDtypeStruct((e, cap_p, d), F32),
        compiler_params=_cparams(2),
        name="expert_stage",
    )(xs, gates, wg, wu, wd)


def _q_head_perm():
    groups = N_HEADS // N_KV_HEADS
    cols = []
    for m in range(N_KV_HEADS // 2):
        for i in range(groups):
            for h in (2 * groups * m + i, 2 * groups * m + groups + i):
                cols.extend(range(h * HEAD_DIM, (h + 1) * HEAD_DIM))
    return np.asarray(cols, np.int32)


def _rope_tables(s):
    pos = jnp.concatenate([N_META + jnp.arange(s, dtype=F32), jnp.zeros((META_ROW0,), F32),
                           jnp.arange(N_META, dtype=F32)])
    inv = ROPE_THETA ** (-jnp.arange(0, HEAD_DIM, 2, dtype=F32) / HEAD_DIM)
    ang = pos[:, None] * inv[None, :]
    reps = LANES // (HEAD_DIM // 2)
    cos2 = jnp.tile(jnp.cos(ang), (1, reps))
    sgn = jnp.where((jnp.arange(LANES) % HEAD_DIM) < HEAD_DIM // 2, -1.0, 1.0).astype(F32)
    sin2 = jnp.tile(jnp.sin(ang), (1, reps)) * sgn[None, :]
    return cos2, sin2


def _sink_rows(sink):
    groups = N_HEADS // N_KV_HEADS
    heads = np.asarray([[2 * groups * m + groups * half + i
                         for i in range(groups) for half in range(2)]
                        for m in range(N_KV_HEADS // 2)], np.int32)
    rows = jnp.repeat(sink.astype(F32)[heads], BLK, axis=1)
    return rows[:, :, None]


def _route(aff_t, s):
    b, n_exp, li = aff_t.shape
    l = s + N_META
    n_tok = b * l
    cap = max(1, CAPACITY_FACTOR * n_tok // n_exp)
    aff_pos = jnp.concatenate([aff_t[:, :, s + META_ROW0:], aff_t[:, :, :s]], axis=2)
    aff_pos = jnp.transpose(aff_pos, (1, 0, 2)).reshape(n_exp, n_tok)
    gate_vals, tok_idx = lax.top_k(aff_pos, cap)
    bi = tok_idx // l
    p = tok_idx % l
    rows = bi * li + jnp.where(p < N_META, s + META_ROW0 + p, p - N_META)
    cap_p = -(-cap // TM) * TM
    rows = jnp.pad(rows, ((0, 0), (0, cap_p - cap)))
    gate_vals = jnp.pad(gate_vals, ((0, 0), (0, cap_p - cap)))
    return rows, gate_vals


def _encoder(x, meta_blk, prm):
    b, s, d = x.shape
    assert s % TM == 0 and d % LANES == 0
    li = s + BLK
    cos2, sin2 = _rope_tables(s)
    q, k, v, cb, sprod, ga, gc = _stage_a(x, meta_blk, cos2, sin2, prm["gmix"], prm["gq2"],
                                          prm["gk2"], prm["bd"], prm["w_in"])
    attn = _stage_b(q, k, v, prm["sink_rows"])
    x1, h2, aff_t = _stage_c(x, meta_blk, attn, cb, sprod, ga, gc, prm["wconv"], prm["gffn"],
                             prm["wa"], prm["wc"], prm["wo"], prm["wr_t"])
    rows, gate_vals = _route(aff_t, s)
    h2f = h2.reshape(b * li, d)
    xs = h2f[rows]
    ye = _stage_d(xs, gate_vals[:, :, None], prm["wg"], prm["wu"], prm["wd"])
    ffn = jnp.zeros((b * li, d), F32).at[rows.reshape(-1)].add(ye.reshape(-1, d))
    return (x1 + ffn.reshape(b, li, d))[:, :s]


def kernel(x_prompt, x_sample, meta_tokens, g_mix, w_in, g_q, g_k, sink_logits, w_conv, w_attn_out,
           w_conv_out, w_out, g_ffn, w_router, w_expert_gate, w_expert_up, w_expert_down):
    assert w_in.shape[0] == 1, "single-layer block"
    d = x_prompt.shape[-1]
    qw = N_HEADS * HEAD_DIM
    perm = _q_head_perm()
    w_in0 = w_in[0]
    lane = np.arange(LANES)
    prm = {
        "w_in": jnp.concatenate([w_in0[:, :qw][:, perm], w_in0[:, qw:]], axis=1).astype(BF16),
        "gmix": g_mix[0][None, :].astype(F32),
        "gq2": jnp.tile(g_q[0], LANES // HEAD_DIM)[None, :].astype(F32),
        "gk2": jnp.tile(g_k[0], LANES // HEAD_DIM)[None, :].astype(F32),
        "bd": jnp.asarray((lane[:, None] // HEAD_DIM) == (lane[None, :] // HEAD_DIM), BF16),
        "sink_rows": _sink_rows(sink_logits[0]),
        "wconv": w_conv[0].astype(F32),
        "gffn": g_ffn[0][None, :].astype(F32),
        "wa": w_attn_out[0][perm, :].astype(BF16),
        "wc": w_conv_out[0].astype(BF16),
        "wo": w_out[0].astype(BF16),
        "wr_t": jnp.transpose(w_router[0]).astype(BF16),
        "wg": w_expert_gate[0].astype(BF16),
        "wu": w_expert_up[0].astype(BF16),
        "wd": w_expert_down[0].astype(BF16),
    }
    meta_blk = jnp.concatenate([jnp.zeros((META_ROW0, d), F32), meta_tokens.astype(F32)], axis=0)
    return (_encoder(x_prompt, meta_blk, prm), _encoder(x_sample, meta_blk, prm))
```

```python
import functools

import numpy as np
import jax
import jax.numpy as jnp
from jax import lax
from jax.experimental import pallas as pl
from jax.experimental.pallas import tpu as pltpu

N_META = 16
N_HEADS = 16
N_KV_HEADS = 4
HEAD_DIM = 64
N_EXPERTS = 16
CAPACITY_FACTOR = 2
ROPE_THETA = 10000.0
EPS = 1e-6
NEG = -1e30

LANES = 128
BLK = 128
META_ROW0 = BLK - N_META
TM = 512
HALO = 16
DEN_ROWS = 16
ATTN_LOOKAHEAD = 4
CH = 128
UNSELECTED = -(1 << 30)
LOG2E = 1.4426950408889634
Q_SCALE = HEAD_DIM ** -0.5 * LOG2E
VMEM_LIMIT = 56 * 1024 * 1024

F32 = jnp.float32
BF16 = jnp.bfloat16


def _dot(a, b):
    return jnp.dot(a, b, preferred_element_type=F32)


def _dot_nt(a, b):
    return lax.dot_general(a, b, (((1,), (1,)), ((), ())), preferred_element_type=F32)


def _cparams(n_axes):
    return pltpu.CompilerParams(dimension_semantics=("arbitrary",) * n_axes,
                                vmem_limit_bytes=VMEM_LIMIT)


def _proj_kernel(x_ref, meta_ref, cos_ref, sin_ref, gmix_ref, gq_ref, gk_ref, bd_ref, w_ref,
                 q_ref, k_ref, v_ref, cb_ref, s_ref, ga_ref, gc_ref, *, nreal, d, qw, kvw, cw):
    j = pl.program_id(1)
    lane = lax.broadcasted_iota(jnp.int32, (1, LANES), 1)
    first_half = jnp.bitwise_and(lane, HEAD_DIM - 1) < (HEAD_DIM // 2)

    def norm_rope(t, g, cos, sin):
        ss = _dot((t * t).astype(BF16), bd_ref[...])
        t = t * lax.rsqrt(ss * (1.0 / HEAD_DIM) + EPS) * g
        partner = jnp.where(first_half, pltpu.roll(t, LANES - HEAD_DIM // 2, 1),
                            pltpu.roll(t, HEAD_DIM // 2, 1))
        return t * cos + partner * sin

    def body(x, cos, sin, rows):
        ms = jnp.mean(x * x, axis=-1, keepdims=True)
        h = (x * lax.rsqrt(ms + EPS) * gmix_ref[...]).astype(BF16)
        off = 0
        for c in range(qw // LANES):
            t = _dot(h, w_ref[:, off:off + LANES])
            t = norm_rope(t, gq_ref[...], cos, sin) * Q_SCALE
            q_ref[0, :rows, c * LANES:(c + 1) * LANES] = t.astype(BF16)
            off += LANES
        for c in range(kvw // LANES):
            t = _dot(h, w_ref[:, off:off + LANES])
            k_ref[0, :rows, c * LANES:(c + 1) * LANES] = norm_rope(t, gk_ref[...], cos, sin).astype(BF16)
            off += LANES
        v_ref[0, :rows, :] = _dot(h, w_ref[:, off:off + kvw]).astype(BF16)
        off += kvw
        cb_ref[0, :rows, :] = _dot(h, w_ref[:, off:off + cw]).astype(BF16)
        off += cw
        cc = _dot(h, w_ref[:, off:off + cw])
        off += cw
        cu = _dot(h, w_ref[:, off:off + cw])
        off += cw
        s_ref[0, :rows, :] = (cc * cu).astype(BF16)
        ga_ref[0, :rows, :] = jax.nn.sigmoid(_dot(h, w_ref[:, off:off + d])).astype(BF16)
        off += d
        gc_ref[0, :rows, :] = jax.nn.sigmoid(_dot(h, w_ref[:, off:off + d])).astype(BF16)

    @pl.when(j < nreal)
    def _():
        body(x_ref[0], cos_ref[...], sin_ref[...], TM)

    @pl.when(j == nreal)
    def _():
        body(meta_ref[...], cos_ref[:BLK, :], sin_ref[:BLK, :], BLK)


def _stage_a(x, meta_blk, cos2, sin2, gmix, gq2, gk2, bd, w_in_b):
    b, s, d = x.shape
    li = s + BLK
    nreal = s // TM
    qw, kvw, cw = N_HEADS * HEAD_DIM, N_KV_HEADS * HEAD_DIM, d
    const = lambda bi, j: (0, 0)
    rowblk = lambda bi, j: (bi, j, 0)
    out_w = (qw, kvw, kvw, cw, cw, d, d)
    return pl.pallas_call(
        functools.partial(_proj_kernel, nreal=nreal, d=d, qw=qw, kvw=kvw, cw=cw),
        grid=(b, nreal + 1),
        in_specs=[
            pl.BlockSpec((1, TM, d), lambda bi, j: (bi, jnp.minimum(j, nreal - 1), 0)),
            pl.BlockSpec((BLK, d), const),
            pl.BlockSpec((TM, LANES), lambda bi, j: (j, 0)),
            pl.BlockSpec((TM, LANES), lambda bi, j: (j, 0)),
            pl.BlockSpec((1, d), const),
            pl.BlockSpec((1, LANES), const),
            pl.BlockSpec((1, LANES), const),
            pl.BlockSpec((LANES, LANES), const),
            pl.BlockSpec(w_in_b.shape, const, pipeline_mode=pl.Buffered(1)),
        ],
        out_specs=[pl.BlockSpec((1, TM, w), rowblk) for w in out_w],
        out_shape=[jax.ShapeDtypeStruct((b, li, w), BF16) for w in out_w],
        compiler_params=_cparams(2),
        name="proj_stage",
    )(x, meta_blk, cos2, sin2, gmix, gq2, gk2, bd, w_in_b)


def _attn_kernel(q_ref, kp_ref, kc_ref, kn_ref, vp_ref, vc_ref, vn_ref, sink_ref, o_ref, *, nblk):
    sb = pl.program_id(1)
    kj = lax.broadcasted_iota(jnp.int32, (3 * BLK, BLK), 0)
    qi = lax.broadcasted_iota(jnp.int32, (3 * BLK, BLK), 1)
    band = (kj >= qi) & (kj <= qi + 2 * BLK)
    seg = jnp.right_shift(kj, 7)
    row = jnp.bitwise_and(kj, BLK - 1)
    lo_prev = jnp.where(sb == 0, META_ROW0, jnp.where(sb == nblk - 1, BLK, 0))
    lo_cur = jnp.where(sb == nblk - 1, META_ROW0, 0)
    lo_next = jnp.where(sb == nblk - 2, BLK, 0)
    lo = jnp.where(seg == 0, lo_prev, jnp.where(seg == 1, lo_cur, lo_next))
    ok = band & (row >= lo)
    lane = lax.broadcasted_iota(jnp.int32, (1, LANES), 1)
    first = lane < HEAD_DIM
    first_rows = lax.broadcasted_iota(jnp.int32, (LANES, 1), 0) < HEAD_DIM
    zero = jnp.zeros((), BF16)
    groups = N_HEADS // N_KV_HEADS
    ok2 = jnp.concatenate([ok, ok], axis=1)
    n_pairs = N_KV_HEADS * HEAD_DIM // LANES
    kcat, v_t = [], []
    for m in range(n_pairs):
        cols = slice(m * LANES, (m + 1) * LANES)
        kcat.append(jnp.concatenate([kp_ref[0, :, cols], kc_ref[0, :, cols], kn_ref[0, :, cols]], axis=0))
        vcat = jnp.concatenate([vp_ref[0, :, cols], vc_ref[0, :, cols], vn_ref[0, :, cols]], axis=0)
        v_t.append(jnp.concatenate([jnp.transpose(vcat.astype(F32)).astype(BF16),
                                    jnp.ones((DEN_ROWS, 3 * BLK), BF16)], axis=0))

    def scores(c):
        qc = q_ref[0, :, c * LANES:(c + 1) * LANES]
        lhs = jnp.concatenate([jnp.where(first, qc, zero), jnp.where(first, zero, qc)], axis=0)
        return _dot_nt(kcat[c // groups], lhs)

    n_chunks = n_pairs * groups
    pending = [scores(c) for c in range(ATTN_LOOKAHEAD)]
    for c in range(n_chunks):
        m, i = divmod(c, groups)
        if c + ATTN_LOOKAHEAD < n_chunks:
            pending.append(scores(c + ATTN_LOOKAHEAD))
        s_t = jnp.where(ok2, pending.pop(0), NEG)
        sink = sink_ref[m, :, 2 * i * BLK:(2 * i + 2) * BLK]
        mx = jnp.maximum(jnp.max(s_t, axis=0, keepdims=True), sink)
        p = jnp.exp2(s_t - mx)
        o_t = _dot(v_t[m], p.astype(BF16))
        den = o_t[LANES:LANES + 1, :] + jnp.exp2(sink - mx)
        o_t = o_t[:LANES, :] / den
        blk = jnp.where(first_rows, o_t[:, :BLK], o_t[:, BLK:])
        o_ref[0, :, c * LANES:(c + 1) * LANES] = jnp.transpose(blk).astype(BF16)


def _stage_b(q, k, v, sink_rows):
    b, li, qw = q.shape
    kvw = k.shape[-1]
    nblk = li // BLK
    prev = lambda bi, sb: (bi, (sb + nblk - 1) % nblk, 0)
    cur = lambda bi, sb: (bi, sb, 0)
    nxt = lambda bi, sb: (bi, (sb + 1) % nblk, 0)
    kv_specs = [pl.BlockSpec((1, BLK, kvw), f) for f in (prev, cur, nxt)]
    return pl.pallas_call(
        functools.partial(_attn_kernel, nblk=nblk),
        grid=(b, nblk),
        in_specs=[pl.BlockSpec((1, BLK, qw), cur)] + kv_specs + kv_specs
        + [pl.BlockSpec(sink_rows.shape, lambda bi, sb: (0, 0, 0))],
        out_specs=pl.BlockSpec((1, BLK, qw), cur),
        out_shape=jax.ShapeDtypeStruct((b, li, qw), BF16),
        compiler_params=_cparams(2),
        name="attn_stage",
    )(q, k, k, k, v, v, v, sink_rows)


def _mix_kernel(x_ref, meta_ref, attn_ref, cb_ref, s_ref, sp_ref, sn_ref, ga_ref, gc_ref,
                wconv_ref, gffn_ref, wa_ref, wc_ref, wo_ref, wr_ref,
                x1_ref, h2_ref, aff_ref, *, nreal):
    j = pl.program_id(1)

    def body(x, rows):
        s = s_ref[0, :rows, :].astype(F32)
        ridx = lax.broadcasted_iota(jnp.int32, (rows, 1), 0)
        s_prev = jnp.where(ridx == 0, sp_ref[0, HALO - 1:HALO, :].astype(F32), pltpu.roll(s, 1, 0))
        s_next = jnp.where(ridx == rows - 1, sn_ref[0, 0:1, :].astype(F32), pltpu.roll(s, rows - 1, 0))
        conv = wconv_ref[0:1, :] * s_prev + wconv_ref[1:2, :] * s + wconv_ref[2:3, :] * s_next
        conv_in = (cb_ref[0, :rows, :].astype(F32) * conv).astype(BF16)
        conv_p = _dot(conv_in, wc_ref[...])
        attn_p = _dot(attn_ref[0, :rows, :], wa_ref[...])
        merged = ga_ref[0, :rows, :].astype(F32) * attn_p + gc_ref[0, :rows, :].astype(F32) * conv_p
        x1 = x + _dot(merged.astype(BF16), wo_ref[...])
        x1_ref[0, :rows, :] = x1
        ms = jnp.mean(x1 * x1, axis=-1, keepdims=True)
        h2 = (x1 * lax.rsqrt(ms + EPS) * gffn_ref[...]).astype(BF16)
        h2_ref[0, :rows, :] = h2
        logits = _dot_nt(wr_ref[...], h2)
        e = jnp.exp(logits - jnp.max(logits, axis=0, keepdims=True))
        aff_ref[0, :, :rows] = e / jnp.sum(e, axis=0, keepdims=True)

    @pl.when(j < nreal)
    def _():
        body(x_ref[0], TM)

    @pl.when(j == nreal)
    def _():
        body(meta_ref[...], BLK)


def _stage_c(x, meta_blk, attn, cb, sprod, ga, gc, wconv, gffn, wa, wc, wo, wr_t):
    b, s, d = x.shape
    li = s + BLK
    nreal = s // TM
    nhalo = li // HALO
    const = lambda bi, j: (0, 0)
    rowblk = lambda bi, j: (bi, j, 0)
    halo_prev = lambda bi, j: (bi, (j * (TM // HALO) + nhalo - 1) % nhalo, 0)
    halo_next = lambda bi, j: (bi, jnp.where(j == nreal, 0, (j + 1) * (TM // HALO)), 0)
    n_exp = wr_t.shape[0]
    wspec = lambda w: pl.BlockSpec(w.shape, const, pipeline_mode=pl.Buffered(1))
    return pl.pallas_call(
        functools.partial(_mix_kernel, nreal=nreal),
        grid=(b, nreal + 1),
        in_specs=[
            pl.BlockSpec((1, TM, d), lambda bi, j: (bi, jnp.minimum(j, nreal - 1), 0)),
            pl.BlockSpec((BLK, d), const),
            pl.BlockSpec((1, TM, d), rowblk),
            pl.BlockSpec((1, TM, d), rowblk),
            pl.BlockSpec((1, TM, d), rowblk),
            pl.BlockSpec((1, HALO, d), halo_prev),
            pl.BlockSpec((1, HALO, d), halo_next),
            pl.BlockSpec((1, TM, d), rowblk),
            pl.BlockSpec((1, TM, d), rowblk),
            pl.BlockSpec(wconv.shape, const),
            pl.BlockSpec((1, d), const),
            wspec(wa), wspec(wc), wspec(wo), wspec(wr_t),
        ],
        out_specs=[
            pl.BlockSpec((1, TM, d), rowblk),
            pl.BlockSpec((1, TM, d), rowblk),
            pl.BlockSpec((1, n_exp, TM), lambda bi, j: (bi, 0, j)),
        ],
        out_shape=[
            jax.ShapeDtypeStruct((b, li, d), F32),
            jax.ShapeDtypeStruct((b, li, d), BF16),
            jax.ShapeDtypeStruct((b, n_exp, li), F32),
        ],
        compiler_params=_cparams(2),
        name="mix_stage",
    )(x, meta_blk, attn, cb, sprod, sprod, sprod, ga, gc, wconv, gffn, wa, wc, wo, wr_t)


def _expert_kernel(xs_ref, gate_ref, wg_ref, wu_ref, wd_ref, y_ref):
    x = xs_ref[0]
    g = _dot(x, wg_ref[0])
    u = _dot(x, wu_ref[0])
    he = (g * jax.nn.sigmoid(g) * u).astype(BF16)
    y_ref[0] = (_dot(he, wd_ref[0]) * gate_ref[0]).astype(y_ref.dtype)


def _stage_d(xs, gates, wg, wu, wd):
    e, cap_p, d = xs.shape
    de = wg.shape[-1]
    tile = lambda ei, t: (ei, t, 0)
    wmap = lambda ei, t: (ei, 0, 0)
    return pl.pallas_call(
        _expert_kernel,
        grid=(e, cap_p // TM),
        in_specs=[
            pl.BlockSpec((1, TM, d), tile),
            pl.BlockSpec((1, TM, 1), tile),
            pl.BlockSpec((1, d, de), wmap),
            pl.BlockSpec((1, d, de), wmap),
            pl.BlockSpec((1, de, d), wmap),
        ],
        out_specs=pl.BlockSpec((1, TM, d), tile),
        out_shape=jax.ShapeDtypeStruct((e, cap_p, d), BF16),
        compiler_params=_cparams(2),
        name="expert_stage",
    )(xs, gates, wg, wu, wd)


def _combine_kernel(start_ref, rounds_ref, x1_ref, slot_ref, ye_ref, o_ref, ybuf, sem, *, n_exp, nsteps):
    lin = pl.program_id(0) * pl.num_programs(1) + pl.program_id(1)

    def chunk_copy(step, e, rnd, buf):
        st = pl.multiple_of(start_ref[step * n_exp + e] + rnd * CH, HALO)
        return pltpu.make_async_copy(ye_ref.at[e, pl.ds(st, CH), :],
                                     ybuf.at[buf, pl.ds(e * CH, CH), :], sem.at[buf])

    def fetch(step, rnd, buf):
        for e in range(n_exp):
            chunk_copy(step, e, rnd, buf).start()

    def wait(step, rnd, buf):
        for e in range(n_exp):
            chunk_copy(step, e, rnd, buf).wait()

    buf = lin % 2

    @pl.when(lin == 0)
    def _():
        fetch(0, 0, 0)

    @pl.when(lin + 1 < nsteps)
    def _():
        fetch(lin + 1, 0, 1 - buf)

    slot = slot_ref[0]
    lane_c = lax.broadcasted_iota(jnp.int32, (1, CH), 1)

    def gathered(rnd):
        onehot = [(slot[:, e:e + 1] - (start_ref[lin * n_exp + e] + rnd * CH) == lane_c).astype(BF16)
                  for e in range(n_exp)]
        return _dot(jnp.concatenate(onehot, axis=1), ybuf[buf])

    wait(lin, 0, buf)
    o_ref[0] = x1_ref[0] + gathered(0)

    def extra(rnd, carry):
        fetch(lin, rnd, buf)
        wait(lin, rnd, buf)
        o_ref[0] += gathered(rnd)
        return carry

    lax.fori_loop(1, rounds_ref[lin], extra, 0)


def _stage_e(x1, slot, ye, start_al, rounds, s):
    b, li, d = x1.shape
    n_exp = slot.shape[-1]
    ntile = s // TM
    tile = lambda bi, j, *_: (bi, j, 0)
    return pl.pallas_call(
        functools.partial(_combine_kernel, n_exp=n_exp, nsteps=b * ntile),
        grid_spec=pltpu.PrefetchScalarGridSpec(
            num_scalar_prefetch=2,
            grid=(b, ntile),
            in_specs=[
                pl.BlockSpec((1, TM, d), tile),
                pl.BlockSpec((1, TM, n_exp), tile),
                pl.BlockSpec(memory_space=pl.ANY),
            ],
            out_specs=pl.BlockSpec((1, TM, d), tile),
            scratch_shapes=[pltpu.VMEM((2, n_exp * CH, d), BF16), pltpu.SemaphoreType.DMA((2,))],
        ),
        out_shape=jax.ShapeDtypeStruct((b, s, d), F32),
        compiler_params=_cparams(2),
        name="combine_stage",
    )(start_al, rounds, x1, slot, ye)


def _q_head_perm():
    groups = N_HEADS // N_KV_HEADS
    cols = []
    for m in range(N_KV_HEADS // 2):
        for i in range(groups):
            for h in (2 * groups * m + i, 2 * groups * m + groups + i):
                cols.extend(range(h * HEAD_DIM, (h + 1) * HEAD_DIM))
    return np.asarray(cols, np.int32)


def _rope_tables(s):
    pos = jnp.concatenate([N_META + jnp.arange(s, dtype=F32), jnp.zeros((META_ROW0,), F32),
                           jnp.arange(N_META, dtype=F32)])
    inv = ROPE_THETA ** (-jnp.arange(0, HEAD_DIM, 2, dtype=F32) / HEAD_DIM)
    ang = pos[:, None] * inv[None, :]
    reps = LANES // (HEAD_DIM // 2)
    cos2 = jnp.tile(jnp.cos(ang), (1, reps))
    sgn = jnp.where((jnp.arange(LANES) % HEAD_DIM) < HEAD_DIM // 2, -1.0, 1.0).astype(F32)
    sin2 = jnp.tile(jnp.sin(ang), (1, reps)) * sgn[None, :]
    return cos2, sin2


def _sink_rows(sink):
    groups = N_HEADS // N_KV_HEADS
    heads = np.asarray([[2 * groups * m + groups * half + i
                         for i in range(groups) for half in range(2)]
                        for m in range(N_KV_HEADS // 2)], np.int32)
    rows = jnp.repeat(sink.astype(F32)[heads] * LOG2E, BLK, axis=1)
    return rows[:, None, :]


def _route(aff_t, s):
    b, n_exp, li = aff_t.shape
    l = s + N_META
    n_tok = b * l
    cap = max(1, CAPACITY_FACTOR * n_tok // n_exp)
    aff_pos = jnp.concatenate([aff_t[:, :, s + META_ROW0:], aff_t[:, :, :s]], axis=2)
    aff_pos = jnp.transpose(aff_pos, (1, 0, 2)).reshape(n_exp, n_tok)
    gate_vals, tok_idx = lax.top_k(aff_pos, cap)
    thr = gate_vals[:, cap - 1]
    bi = tok_idx // l
    p = tok_idx % l
    rows = bi * li + jnp.where(p < N_META, s + META_ROW0 + p, p - N_META)
    rows, gate_vals = lax.sort((rows, gate_vals), dimension=1, num_keys=1)
    cap_p = -(-(cap + CH) // TM) * TM
    rows = jnp.pad(rows, ((0, 0), (0, cap_p - cap)))
    gate_vals = jnp.pad(gate_vals, ((0, 0), (0, cap_p - cap)))

    above = aff_pos > thr[:, None]
    equal = aff_pos == thr[:, None]
    room = cap - jnp.sum(above, axis=1, dtype=jnp.int32)
    sel = above | (equal & (jnp.cumsum(equal, axis=1, dtype=jnp.int32) <= room[:, None]))
    sel = jnp.transpose(sel.reshape(n_exp, b, l), (1, 0, 2))
    sel = jnp.concatenate([sel[:, :, N_META:], jnp.zeros((b, n_exp, META_ROW0), bool),
                           sel[:, :, :N_META]], axis=2)
    sel = jnp.transpose(sel, (1, 0, 2)).reshape(n_exp, b * li)
    upto = jnp.cumsum(sel, axis=1, dtype=jnp.int32)
    slot = jnp.where(sel, upto - 1, UNSELECTED)
    slot = jnp.transpose(slot).reshape(b, li, n_exp)
    before = (upto - sel).reshape(n_exp, b, li)
    start = before[:, :, 0:s:TM]
    end = before[:, :, TM:s + 1:TM]
    start_al = start // HALO * HALO
    rounds = jnp.maximum(jnp.max(-(-(end - start_al) // CH), axis=0), 1)
    start_al = jnp.transpose(start_al, (1, 2, 0)).reshape(-1)
    return rows, gate_vals, slot, start_al, rounds.reshape(-1)


def _encoder(x, meta_blk, prm):
    b, s, d = x.shape
    assert s % TM == 0 and d % LANES == 0
    li = s + BLK
    cos2, sin2 = _rope_tables(s)
    q, k, v, cb, sprod, ga, gc = _stage_a(x, meta_blk, cos2, sin2, prm["gmix"], prm["gq2"],
                                          prm["gk2"], prm["bd"], prm["w_in"])
    attn = _stage_b(q, k, v, prm["sink_rows"])
    x1, h2, aff_t = _stage_c(x, meta_blk, attn, cb, sprod, ga, gc, prm["wconv"], prm["gffn"],
                             prm["wa"], prm["wc"], prm["wo"], prm["wr_t"])
    rows, gate_vals, slot, start_al, rounds = _route(aff_t, s)
    xs = h2.reshape(b * li, d)[rows]
    ye = _stage_d(xs, gate_vals[:, :, None], prm["wg"], prm["wu"], prm["wd"])
    return _stage_e(x1, slot, ye, start_al, rounds, s)


def kernel(x_prompt, x_sample, meta_tokens, g_mix, w_in, g_q, g_k, sink_logits, w_conv, w_attn_out,
           w_conv_out, w_out, g_ffn, w_router, w_expert_gate, w_expert_up, w_expert_down):
    assert w_in.shape[0] == 1, "single-layer block"
    d = x_prompt.shape[-1]
    qw = N_HEADS * HEAD_DIM
    perm = _q_head_perm()
    w_in0 = w_in[0]
    lane = np.arange(LANES)
    prm = {
        "w_in": jnp.concatenate([w_in0[:, :qw][:, perm], w_in0[:, qw:]], axis=1).astype(BF16),
        "gmix": g_mix[0][None, :].astype(F32),
        "gq2": jnp.tile(g_q[0], LANES // HEAD_DIM)[None, :].astype(F32),
        "gk2": jnp.tile(g_k[0], LANES // HEAD_DIM)[None, :].astype(F32),
        "bd": jnp.asarray((lane[:, None] // HEAD_DIM) == (lane[None, :] // HEAD_DIM), BF16),
        "sink_rows": _sink_rows(sink_logits[0]),
        "wconv": w_conv[0].astype(F32),
        "gffn": g_ffn[0][None, :].astype(F32),
        "wa": w_attn_out[0][perm, :].astype(BF16),
        "wc": w_conv_out[0].astype(BF16),
        "wo": w_out[0].astype(BF16),
        "wr_t": jnp.transpose(w_router[0]).astype(BF16),
        "wg": w_expert_gate[0].astype(BF16),
        "wu": w_expert_up[0].astype(BF16),
        "wd": w_expert_down[0].astype(BF16),
    }
    meta_blk = jnp.concatenate([jnp.zeros((META_ROW0, d), F32), meta_tokens.astype(F32)], axis=0)
    return (_encoder(x_prompt, meta_blk, prm), _encoder(x_sample, meta_blk, prm))
```

```python
import functools

import numpy as np
import jax
import jax.numpy as jnp
from jax import lax
from jax.experimental import pallas as pl
from jax.experimental.pallas import tpu as pltpu

N_META = 16
N_HEADS = 16
N_KV_HEADS = 4
HEAD_DIM = 64
N_EXPERTS = 16
CAPACITY_FACTOR = 2
ROPE_THETA = 10000.0
EPS = 1e-6
NEG = -1e30

LANES = 128
BLK = 128
META_ROW0 = BLK - N_META
TM = 512
HALO = 16
DEN_ROWS = 16
ATTN_LOOKAHEAD = 4
CH = 128
UNSELECTED = -(1 << 30)
LOG2E = 1.4426950408889634
Q_SCALE = HEAD_DIM ** -0.5 * LOG2E
VMEM_LIMIT = 56 * 1024 * 1024

F32 = jnp.float32
BF16 = jnp.bfloat16


def _dot(a, b):
    return jnp.dot(a, b, preferred_element_type=F32)


def _dot_nt(a, b):
    return lax.dot_general(a, b, (((1,), (1,)), ((), ())), preferred_element_type=F32)


def _cparams(n_axes):
    return pltpu.CompilerParams(dimension_semantics=("arbitrary",) * n_axes,
                                vmem_limit_bytes=VMEM_LIMIT)


def _proj_kernel(x_ref, meta_ref, cos_ref, sin_ref, gmix_ref, gq_ref, gk_ref, bd_ref, w_ref,
                 q_ref, k_ref, v_ref, cb_ref, s_ref, ga_ref, gc_ref, *, nreal, d, qw, kvw, cw):
    j = pl.program_id(1)
    lane = lax.broadcasted_iota(jnp.int32, (1, LANES), 1)
    first_half = jnp.bitwise_and(lane, HEAD_DIM - 1) < (HEAD_DIM // 2)

    def norm_rope(t, ss, g, cos, sin):
        t = t * lax.rsqrt(ss * (1.0 / HEAD_DIM) + EPS) * g
        partner = jnp.where(first_half, pltpu.roll(t, LANES - HEAD_DIM // 2, 1),
                            pltpu.roll(t, HEAD_DIM // 2, 1))
        return t * cos + partner * sin

    def body(x, cos, sin, rows):
        ms = jnp.mean(x * x, axis=-1, keepdims=True)
        h = (x * lax.rsqrt(ms + EPS) * gmix_ref[...]).astype(BF16)
        qk = _dot(h, w_ref[:, :qw + kvw])
        off = qw + kvw
        v_ref[0, :rows, :] = _dot(h, w_ref[:, off:off + kvw]).astype(BF16)
        off += kvw
        chunks = [qk[:, c * LANES:(c + 1) * LANES] for c in range((qw + kvw) // LANES)]
        sumsq = [_dot((t * t).astype(BF16), bd_ref[...]) for t in chunks]
        cb_ref[0, :rows, :] = _dot(h, w_ref[:, off:off + cw]).astype(BF16)
        off += cw
        cc = _dot(h, w_ref[:, off:off + cw])
        off += cw
        cu = _dot(h, w_ref[:, off:off + cw])
        off += cw
        s_ref[0, :rows, :] = (cc * cu).astype(BF16)
        ga_ref[0, :rows, :] = jax.nn.sigmoid(_dot(h, w_ref[:, off:off + d])).astype(BF16)
        off += d
        gc_ref[0, :rows, :] = jax.nn.sigmoid(_dot(h, w_ref[:, off:off + d])).astype(BF16)
        for c, (t, ss) in enumerate(zip(chunks, sumsq)):
            is_q = c < qw // LANES
            t = norm_rope(t, ss, (gq_ref if is_q else gk_ref)[...], cos, sin)
            if is_q:
                q_ref[0, :rows, c * LANES:(c + 1) * LANES] = (t * Q_SCALE).astype(BF16)
            else:
                ck = c - qw // LANES
                k_ref[0, :rows, ck * LANES:(ck + 1) * LANES] = t.astype(BF16)

    @pl.when(j < nreal)
    def _():
        body(x_ref[0], cos_ref[...], sin_ref[...], TM)

    @pl.when(j == nreal)
    def _():
        body(meta_ref[...], cos_ref[:BLK, :], sin_ref[:BLK, :], BLK)


def _stage_a(x, meta_blk, cos2, sin2, gmix, gq2, gk2, bd, w_in_b):
    b, s, d = x.shape
    li = s + BLK
    nreal = s // TM
    qw, kvw, cw = N_HEADS * HEAD_DIM, N_KV_HEADS * HEAD_DIM, d
    const = lambda bi, j: (0, 0)
    rowblk = lambda bi, j: (bi, j, 0)
    out_w = (qw, kvw, kvw, cw, cw, d, d)
    return pl.pallas_call(
        functools.partial(_proj_kernel, nreal=nreal, d=d, qw=qw, kvw=kvw, cw=cw),
        grid=(b, nreal + 1),
        in_specs=[
            pl.BlockSpec((1, TM, d), lambda bi, j: (bi, jnp.minimum(j, nreal - 1), 0)),
            pl.BlockSpec((BLK, d), const),
            pl.BlockSpec((TM, LANES), lambda bi, j: (j, 0)),
            pl.BlockSpec((TM, LANES), lambda bi, j: (j, 0)),
            pl.BlockSpec((1, d), const),
            pl.BlockSpec((1, LANES), const),
            pl.BlockSpec((1, LANES), const),
            pl.BlockSpec((LANES, LANES), const),
            pl.BlockSpec(w_in_b.shape, const, pipeline_mode=pl.Buffered(1)),
        ],
        out_specs=[pl.BlockSpec((1, TM, w), rowblk) for w in out_w],
        out_shape=[jax.ShapeDtypeStruct((b, li, w), BF16) for w in out_w],
        compiler_params=_cparams(2),
        name="proj_stage",
    )(x, meta_blk, cos2, sin2, gmix, gq2, gk2, bd, w_in_b)


def _attn_kernel(q_ref, kp_ref, kc_ref, kn_ref, vp_ref, vc_ref, vn_ref, sink_ref, o_ref, *, nblk):
    sb = pl.program_id(1)
    kj = lax.broadcasted_iota(jnp.int32, (3 * BLK, BLK), 0)
    qi = lax.broadcasted_iota(jnp.int32, (3 * BLK, BLK), 1)
    band = (kj >= qi) & (kj <= qi + 2 * BLK)
    seg = jnp.right_shift(kj, 7)
    row = jnp.bitwise_and(kj, BLK - 1)
    lo_prev = jnp.where(sb == 0, META_ROW0, jnp.where(sb == nblk - 1, BLK, 0))
    lo_cur = jnp.where(sb == nblk - 1, META_ROW0, 0)
    lo_next = jnp.where(sb == nblk - 2, BLK, 0)
    lo = jnp.where(seg == 0, lo_prev, jnp.where(seg == 1, lo_cur, lo_next))
    ok = band & (row >= lo)
    lane = lax.broadcasted_iota(jnp.int32, (1, LANES), 1)
    first = lane < HEAD_DIM
    first_rows = lax.broadcasted_iota(jnp.int32, (LANES, 1), 0) < HEAD_DIM
    zero = jnp.zeros((), BF16)
    groups = N_HEADS // N_KV_HEADS
    ok2 = jnp.concatenate([ok, ok], axis=1)
    n_pairs = N_KV_HEADS * HEAD_DIM // LANES
    kcat, v_t = [], []
    for m in range(n_pairs):
        cols = slice(m * LANES, (m + 1) * LANES)
        kcat.append(jnp.concatenate([kp_ref[0, :, cols], kc_ref[0, :, cols], kn_ref[0, :, cols]], axis=0))
        vcat = jnp.concatenate([vp_ref[0, :, cols], vc_ref[0, :, cols], vn_ref[0, :, cols]], axis=0)
        v_t.append(jnp.concatenate([jnp.transpose(vcat.astype(F32)).astype(BF16),
                                    jnp.ones((DEN_ROWS, 3 * BLK), BF16)], axis=0))

    def scores(c):
        qc = q_ref[0, :, c * LANES:(c + 1) * LANES]
        lhs = jnp.concatenate([jnp.where(first, qc, zero), jnp.where(first, zero, qc)], axis=0)
        return _dot_nt(kcat[c // groups], lhs)

    n_chunks = n_pairs * groups
    pending = [scores(c) for c in range(ATTN_LOOKAHEAD)]
    for c in range(n_chunks):
        m, i = divmod(c, groups)
        if c + ATTN_LOOKAHEAD < n_chunks:
            pending.append(scores(c + ATTN_LOOKAHEAD))
        s_t = jnp.where(ok2, pending.pop(0), NEG)
        sink = sink_ref[m, :, 2 * i * BLK:(2 * i + 2) * BLK]
        mx = jnp.maximum(jnp.max(s_t, axis=0, keepdims=True), sink)
        p = jnp.exp2(s_t - mx)
        o_t = _dot(v_t[m], p.astype(BF16))
        den = o_t[LANES:LANES + 1, :] + jnp.exp2(sink - mx)
        o_t = o_t[:LANES, :] / den
        blk = jnp.where(first_rows, o_t[:, :BLK], o_t[:, BLK:])
        o_ref[0, :, c * LANES:(c + 1) * LANES] = jnp.transpose(blk).astype(BF16)


def _stage_b(q, k, v, sink_rows):
    b, li, qw = q.shape
    kvw = k.shape[-1]
    nblk = li // BLK
    prev = lambda bi, sb: (bi, (sb + nblk - 1) % nblk, 0)
    cur = lambda bi, sb: (bi, sb, 0)
    nxt = lambda bi, sb: (bi, (sb + 1) % nblk, 0)
    kv_specs = [pl.BlockSpec((1, BLK, kvw), f) for f in (prev, cur, nxt)]
    return pl.pallas_call(
        functools.partial(_attn_kernel, nblk=nblk),
        grid=(b, nblk),
        in_specs=[pl.BlockSpec((1, BLK, qw), cur)] + kv_specs + kv_specs
        + [pl.BlockSpec(sink_rows.shape, lambda bi, sb: (0, 0, 0))],
        out_specs=pl.BlockSpec((1, BLK, qw), cur),
        out_shape=jax.ShapeDtypeStruct((b, li, qw), BF16),
        compiler_params=_cparams(2),
        name="attn_stage",
    )(q, k, k, k, v, v, v, sink_rows)


def _mix_kernel(x_ref, meta_ref, attn_ref, cb_ref, s_ref, sp_ref, sn_ref, ga_ref, gc_ref,
                wconv_ref, gffn_ref, wa_ref, wc_ref, wo_ref, wr_ref,
                x1_ref, h2_ref, aff_ref, *, nreal):
    j = pl.program_id(1)

    def body(x, rows, n_pad):
        attn_p = _dot(attn_ref[0, :rows, :], wa_ref[...])
        s = s_ref[0, :rows, :].astype(F32)
        ridx = lax.broadcasted_iota(jnp.int32, (rows, 1), 0)
        s_prev = jnp.where(ridx == 0, sp_ref[0, HALO - 1:HALO, :].astype(F32), pltpu.roll(s, 1, 0))
        s_next = jnp.where(ridx == rows - 1, sn_ref[0, 0:1, :].astype(F32), pltpu.roll(s, rows - 1, 0))
        conv = wconv_ref[0:1, :] * s_prev + wconv_ref[1:2, :] * s + wconv_ref[2:3, :] * s_next
        conv_in = (cb_ref[0, :rows, :].astype(F32) * conv).astype(BF16)
        conv_p = _dot(conv_in, wc_ref[...])
        merged = ga_ref[0, :rows, :].astype(F32) * attn_p + gc_ref[0, :rows, :].astype(F32) * conv_p
        x1 = x + _dot(merged.astype(BF16), wo_ref[...])
        x1_ref[0, :rows, :] = x1
        ms = jnp.mean(x1 * x1, axis=-1, keepdims=True)
        h2 = (x1 * lax.rsqrt(ms + EPS) * gffn_ref[...]).astype(BF16)
        h2_ref[0, :rows, :] = h2
        logits = _dot_nt(wr_ref[...], h2)
        e = jnp.exp(logits - jnp.max(logits, axis=0, keepdims=True))
        aff = e / jnp.sum(e, axis=0, keepdims=True)
        if n_pad:
            aff = jnp.where(lax.broadcasted_iota(jnp.int32, aff.shape, 1) < n_pad, -1.0, aff)
        aff_ref[0, :, :rows] = aff

    @pl.when(j < nreal)
    def _():
        body(x_ref[0], TM, 0)

    @pl.when(j == nreal)
    def _():
        body(meta_ref[...], BLK, META_ROW0)


def _stage_c(x, meta_blk, attn, cb, sprod, ga, gc, wconv, gffn, wa, wc, wo, wr_t):
    b, s, d = x.shape
    li = s + BLK
    nreal = s // TM
    nhalo = li // HALO
    const = lambda bi, j: (0, 0)
    rowblk = lambda bi, j: (bi, j, 0)
    halo_prev = lambda bi, j: (bi, (j * (TM // HALO) + nhalo - 1) % nhalo, 0)
    halo_next = lambda bi, j: (bi, jnp.where(j == nreal, 0, (j + 1) * (TM // HALO)), 0)
    n_exp = wr_t.shape[0]
    wspec = lambda w: pl.BlockSpec(w.shape, const, pipeline_mode=pl.Buffered(1))
    return pl.pallas_call(
        functools.partial(_mix_kernel, nreal=nreal),
        grid=(b, nreal + 1),
        in_specs=[
            pl.BlockSpec((1, TM, d), lambda bi, j: (bi, jnp.minimum(j, nreal - 1), 0)),
            pl.BlockSpec((BLK, d), const),
            pl.BlockSpec((1, TM, d), rowblk),
            pl.BlockSpec((1, TM, d), rowblk),
            pl.BlockSpec((1, TM, d), rowblk),
            pl.BlockSpec((1, HALO, d), halo_prev),
            pl.BlockSpec((1, HALO, d), halo_next),
            pl.BlockSpec((1, TM, d), rowblk),
            pl.BlockSpec((1, TM, d), rowblk),
            pl.BlockSpec(wconv.shape, const),
            pl.BlockSpec((1, d), const),
            wspec(wa), wspec(wc), wspec(wo), wspec(wr_t),
        ],
        out_specs=[
            pl.BlockSpec((1, TM, d), rowblk),
            pl.BlockSpec((1, TM, d), rowblk),
            pl.BlockSpec((1, n_exp, TM), lambda bi, j: (bi, 0, j)),
        ],
        out_shape=[
            jax.ShapeDtypeStruct((b, li, d), F32),
            jax.ShapeDtypeStruct((b, li, d), BF16),
            jax.ShapeDtypeStruct((b, n_exp, li), F32),
        ],
        compiler_params=_cparams(2),
        name="mix_stage",
    )(x, meta_blk, attn, cb, sprod, sprod, sprod, ga, gc, wconv, gffn, wa, wc, wo, wr_t)


def _expert_kernel(xs_ref, wg_ref, wu_ref, wd_ref, y_ref):
    x = xs_ref[0]
    g = _dot(x, wg_ref[0])
    u = _dot(x, wu_ref[0])
    he = (g * jax.nn.sigmoid(g) * u).astype(BF16)
    y_ref[0] = _dot(he, wd_ref[0]).astype(y_ref.dtype)


def _stage_d(xs, wg, wu, wd):
    e, cap_p, d = xs.shape
    de = wg.shape[-1]
    tile = lambda ei, t: (ei, t, 0)
    wmap = lambda ei, t: (ei, 0, 0)
    return pl.pallas_call(
        _expert_kernel,
        grid=(e, cap_p // TM),
        in_specs=[
            pl.BlockSpec((1, TM, d), tile),
            pl.BlockSpec((1, d, de), wmap),
            pl.BlockSpec((1, d, de), wmap),
            pl.BlockSpec((1, de, d), wmap),
        ],
        out_specs=pl.BlockSpec((1, TM, d), tile),
        out_shape=jax.ShapeDtypeStruct((e, cap_p, d), BF16),
        compiler_params=_cparams(2),
        name="expert_stage",
    )(xs, wg, wu, wd)


def _combine_kernel(start_ref, rounds_ref, x1_ref, slot_ref, gate_ref, ye_ref, o_ref, ybuf, sem, *,
                    n_exp, nsteps):
    lin = pl.program_id(0) * pl.num_programs(1) + pl.program_id(1)

    def chunk_copy(step, e, rnd, buf):
        st = pl.multiple_of(start_ref[step * n_exp + e] + rnd * CH, HALO)
        return pltpu.make_async_copy(ye_ref.at[e, pl.ds(st, CH), :],
                                     ybuf.at[buf, pl.ds(e * CH, CH), :], sem.at[buf])

    def fetch(step, rnd, buf):
        for e in range(n_exp):
            chunk_copy(step, e, rnd, buf).start()

    def wait(step, rnd, buf):
        for e in range(n_exp):
            chunk_copy(step, e, rnd, buf).wait()

    buf = lin % 2

    @pl.when(lin == 0)
    def _():
        fetch(0, 0, 0)

    @pl.when(lin + 1 < nsteps)
    def _():
        fetch(lin + 1, 0, 1 - buf)

    slot = slot_ref[0]
    gate = gate_ref[0]
    lane_c = lax.broadcasted_iota(jnp.int32, (1, CH), 1)

    def gathered(rnd):
        perm = [jnp.where(slot[:, e:e + 1] - (start_ref[lin * n_exp + e] + rnd * CH) == lane_c,
                          gate[:, e:e + 1], 0.0).astype(BF16) for e in range(n_exp)]
        return _dot(jnp.concatenate(perm, axis=1), ybuf[buf])

    wait(lin, 0, buf)
    o_ref[0] = x1_ref[0] + gathered(0)

    def extra(rnd, carry):
        fetch(lin, rnd, buf)
        wait(lin, rnd, buf)
        o_ref[0] += gathered(rnd)
        return carry

    lax.fori_loop(1, rounds_ref[lin], extra, 0)


def _stage_e(x1, slot, gate, ye, start_al, rounds, s):
    b, li, d = x1.shape
    n_exp = slot.shape[-1]
    ntile = s // TM
    tile = lambda bi, j, *_: (bi, j, 0)
    return pl.pallas_call(
        functools.partial(_combine_kernel, n_exp=n_exp, nsteps=b * ntile),
        grid_spec=pltpu.PrefetchScalarGridSpec(
            num_scalar_prefetch=2,
            grid=(b, ntile),
            in_specs=[
                pl.BlockSpec((1, TM, d), tile),
                pl.BlockSpec((1, TM, n_exp), tile),
                pl.BlockSpec((1, TM, n_exp), tile),
                pl.BlockSpec(memory_space=pl.ANY),
            ],
            out_specs=pl.BlockSpec((1, TM, d), tile),
            scratch_shapes=[pltpu.VMEM((2, n_exp * CH, d), BF16), pltpu.SemaphoreType.DMA((2,))],
        ),
        out_shape=jax.ShapeDtypeStruct((b, s, d), F32),
        compiler_params=_cparams(2),
        name="combine_stage",
    )(start_al, rounds, x1, slot, gate, ye)


def _q_head_perm():
    groups = N_HEADS // N_KV_HEADS
    cols = []
    for m in range(N_KV_HEADS // 2):
        for i in range(groups):
            for h in (2 * groups * m + i, 2 * groups * m + groups + i):
                cols.extend(range(h * HEAD_DIM, (h + 1) * HEAD_DIM))
    return np.asarray(cols, np.int32)


def _rope_tables(s):
    pos = jnp.concatenate([N_META + jnp.arange(s, dtype=F32), jnp.zeros((META_ROW0,), F32),
                           jnp.arange(N_META, dtype=F32)])
    inv = ROPE_THETA ** (-jnp.arange(0, HEAD_DIM, 2, dtype=F32) / HEAD_DIM)
    ang = pos[:, None] * inv[None, :]
    reps = LANES // (HEAD_DIM // 2)
    cos2 = jnp.tile(jnp.cos(ang), (1, reps))
    sgn = jnp.where((jnp.arange(LANES) % HEAD_DIM) < HEAD_DIM // 2, -1.0, 1.0).astype(F32)
    sin2 = jnp.tile(jnp.sin(ang), (1, reps)) * sgn[None, :]
    return cos2, sin2


def _sink_rows(sink):
    groups = N_HEADS // N_KV_HEADS
    heads = np.asarray([[2 * groups * m + groups * half + i
                         for i in range(groups) for half in range(2)]
                        for m in range(N_KV_HEADS // 2)], np.int32)
    rows = jnp.repeat(sink.astype(F32)[heads] * LOG2E, BLK, axis=1)
    return rows[:, None, :]


def _threshold_kernel(aff_ref, thr_ref, *, cap):
    bits = pltpu.bitcast(aff_ref[...], jnp.int32)

    def step(i, prefix):
        cand = prefix | jnp.left_shift(jnp.int32(1), 30 - i)
        count = jnp.sum((bits >= cand).astype(jnp.int32), axis=1, keepdims=True)
        return jnp.where(count >= cap, cand, prefix)

    prefix = lax.fori_loop(0, 31, step, jnp.zeros((bits.shape[0], 1), jnp.int32))
    thr_ref[...] = jnp.broadcast_to(pltpu.bitcast(prefix, F32), thr_ref.shape)


def _thresholds(aff, cap):
    n_exp = aff.shape[0]
    out = pl.pallas_call(
        functools.partial(_threshold_kernel, cap=cap),
        out_shape=jax.ShapeDtypeStruct((n_exp, LANES), F32),
        compiler_params=pltpu.CompilerParams(vmem_limit_bytes=VMEM_LIMIT),
        name="threshold_stage",
    )(aff)
    return out[:, 0]


def _route(aff_t, s):
    b, n_exp, li = aff_t.shape
    n = b * li
    n_tok = b * (s + N_META)
    cap = max(1, CAPACITY_FACTOR * n_tok // n_exp)
    cap_p = -(-(cap + CH) // TM) * TM
    aff = jnp.transpose(aff_t, (1, 0, 2)).reshape(n_exp, n)
    thr = _thresholds(aff, cap)[:, None]
    above = aff > thr
    equal = aff == thr
    room = cap - jnp.sum(above, axis=1, dtype=jnp.int32, keepdims=True)
    eq_upto = jnp.cumsum(equal, axis=1, dtype=jnp.int32).reshape(n_exp, b, li)
    eq_real_end = eq_upto[:, :, s - 1:s]
    eq_before_seq = eq_upto[:, :, 0:1] - equal.reshape(n_exp, b, li)[:, :, 0:1]
    eq_real = eq_real_end - eq_before_seq
    eq_meta = eq_upto[:, :, li - 1:li] - eq_real_end
    is_real = (jnp.arange(li) < s)[None, None, :]
    rank = (eq_upto + jnp.where(is_real, eq_meta, -eq_real)).reshape(n_exp, n)
    sel = above | (equal & (rank <= room))

    upto = jnp.cumsum(sel, axis=1, dtype=jnp.int32)
    before = upto - sel
    slot = jnp.transpose(jnp.where(sel, upto - 1, UNSELECTED)).reshape(b, li, n_exp)
    gate = jnp.transpose(jnp.where(sel, aff, 0.0)).reshape(b, li, n_exp)

    before_seq = before.reshape(n_exp, b, li)
    start = before_seq[:, :, 0:s:TM]
    end = before_seq[:, :, TM:s + 1:TM]
    start_al = start // HALO * HALO
    rounds = jnp.maximum(jnp.max(-(-(end - start_al) // CH), axis=0), 1)
    start_al = jnp.transpose(start_al, (1, 2, 0)).reshape(-1)

    ngrp = n // LANES
    grp_before = before[:, ::LANES]
    grp_end = jnp.concatenate([grp_before[:, 1:], upto[:, -1:]], axis=1)
    local = upto.reshape(n_exp, ngrp, LANES) - grp_before[:, :, None]
    gidx = jnp.arange(ngrp, dtype=jnp.int32)[None, :]
    byte = lambda v: jnp.stack([v // 256, v % 256], axis=-1)
    table = jnp.concatenate([local, byte(grp_before), jnp.broadcast_to(byte(gidx), (n_exp, ngrp, 2))],
                            axis=-1).astype(BF16)
    slots = jnp.arange(cap_p, dtype=jnp.int32)[None, :, None]
    in_grp = ((grp_before[:, None, :] <= slots) & (slots < grp_end[:, None, :])).astype(BF16)
    hit = jnp.einsum("esg,egc->esc", in_grp, table, preferred_element_type=F32).astype(jnp.int32)
    j = slots[:, :, 0] - (hit[:, :, LANES] * 256 + hit[:, :, LANES + 1])
    lane = jnp.sum(hit[:, :, :LANES] <= j[:, :, None], axis=-1, dtype=jnp.int32)
    rows = (hit[:, :, LANES + 2] * 256 + hit[:, :, LANES + 3]) * LANES + lane
    rows = jnp.where(slots[:, :, 0] < cap, rows, 0)
    return rows, slot, gate, start_al, rounds.reshape(-1)


def _encoder(x, meta_blk, prm):
    b, s, d = x.shape
    assert s % TM == 0 and d % LANES == 0
    li = s + BLK
    cos2, sin2 = _rope_tables(s)
    q, k, v, cb, sprod, ga, gc = _stage_a(x, meta_blk, cos2, sin2, prm["gmix"], prm["gq2"],
                                          prm["gk2"], prm["bd"], prm["w_in"])
    attn = _stage_b(q, k, v, prm["sink_rows"])
    x1, h2, aff_t = _stage_c(x, meta_blk, attn, cb, sprod, ga, gc, prm["wconv"], prm["gffn"],
                             prm["wa"], prm["wc"], prm["wo"], prm["wr_t"])
    rows, slot, gate, start_al, rounds = _route(aff_t, s)
    xs = h2.reshape(b * li, d)[rows]
    ye = _stage_d(xs, prm["wg"], prm["wu"], prm["wd"])
    return _stage_e(x1, slot, gate, ye, start_al, rounds, s)


def kernel(x_prompt, x_sample, meta_tokens, g_mix, w_in, g_q, g_k, sink_logits, w_conv, w_attn_out,
           w_conv_out, w_out, g_ffn, w_router, w_expert_gate, w_expert_up, w_expert_down):
    assert w_in.shape[0] == 1, "single-layer block"
    d = x_prompt.shape[-1]
    qw = N_HEADS * HEAD_DIM
    perm = _q_head_perm()
    w_in0 = w_in[0]
    lane = np.arange(LANES)
    prm = {
        "w_in": jnp.concatenate([w_in0[:, :qw][:, perm], w_in0[:, qw:]], axis=1).astype(BF16),
        "gmix": g_mix[0][None, :].astype(F32),
        "gq2": jnp.tile(g_q[0], LANES // HEAD_DIM)[None, :].astype(F32),
        "gk2": jnp.tile(g_k[0], LANES // HEAD_DIM)[None, :].astype(F32),
        "bd": jnp.asarray((lane[:, None] // HEAD_DIM) == (lane[None, :] // HEAD_DIM), BF16),
        "sink_rows": _sink_rows(sink_logits[0]),
        "wconv": w_conv[0].astype(F32),
        "gffn": g_ffn[0][None, :].astype(F32),
        "wa": w_attn_out[0][perm, :].astype(BF16),
        "wc": w_conv_out[0].astype(BF16),
        "wo": w_out[0].astype(BF16),
        "wr_t": jnp.transpose(w_router[0]).astype(BF16),
        "wg": w_expert_gate[0].astype(BF16),
        "wu": w_expert_up[0].astype(BF16),
        "wd": w_expert_down[0].astype(BF16),
    }
    meta_blk = jnp.concatenate([jnp.zeros((META_ROW0, d), F32), meta_tokens.astype(F32)], axis=0)
    return (_encoder(x_prompt, meta_blk, prm), _encoder(x_sample, meta_blk, prm))
```

```python
import functools

import numpy as np
import jax
import jax.numpy as jnp
from jax import lax
from jax.experimental import pallas as pl
from jax.experimental.pallas import tpu as pltpu

N_META = 16
N_HEADS = 16
N_KV_HEADS = 4
HEAD_DIM = 64
N_EXPERTS = 16
CAPACITY_FACTOR = 2
ROPE_THETA = 10000.0
EPS = 1e-6
NEG = -1e30

LANES = 128
BLK = 128
META_ROW0 = BLK - N_META
TM = 512
TE = 528
HALO = 16
DEN_ROWS = 16
ATTN_LOOKAHEAD = 4
CH = 128
UNSELECTED = -(1 << 30)
LOG2E = 1.4426950408889634
Q_SCALE = HEAD_DIM ** -0.5 * LOG2E
VMEM_LIMIT = 56 * 1024 * 1024
CAST_CHUNK_BYTES = 2 * 1024 * 1024

F32 = jnp.float32
BF16 = jnp.bfloat16


def _dot(a, b):
    return jnp.dot(a, b, preferred_element_type=F32)


def _dot_nt(a, b):
    return lax.dot_general(a, b, (((1,), (1,)), ((), ())), preferred_element_type=F32)


def _cparams(n_axes):
    return pltpu.CompilerParams(dimension_semantics=("arbitrary",) * n_axes,
                                vmem_limit_bytes=VMEM_LIMIT)


def _proj_kernel(x_ref, meta_ref, cos_ref, sin_ref, gmix_ref, gq_ref, gk_ref, bd_ref, w_ref, *refs,
                 nreal, d, qw, kvw, cw, n_cast):
    cast_in, refs = refs[:n_cast], refs[n_cast:]
    (q_ref, k_ref, v_ref, cb_ref, s_ref, ga_ref, gc_ref), cast_out = refs[:7], refs[7:]
    j = pl.program_id(1)
    lane = lax.broadcasted_iota(jnp.int32, (1, LANES), 1)
    first_half = jnp.bitwise_and(lane, HEAD_DIM - 1) < (HEAD_DIM // 2)

    def norm_rope(t, ss, g, cos, sin):
        t = t * lax.rsqrt(ss * (1.0 / HEAD_DIM) + EPS) * g
        partner = jnp.where(first_half, pltpu.roll(t, LANES - HEAD_DIM // 2, 1),
                            pltpu.roll(t, HEAD_DIM // 2, 1))
        return t * cos + partner * sin

    def body(x, cos, sin, rows):
        ms = jnp.mean(x * x, axis=-1, keepdims=True)
        h = (x * lax.rsqrt(ms + EPS) * gmix_ref[...]).astype(BF16)
        qk = _dot(h, w_ref[:, :qw + kvw])
        off = qw + kvw
        v_ref[0, :rows, :] = _dot(h, w_ref[:, off:off + kvw]).astype(BF16)
        off += kvw
        chunks = [qk[:, c * LANES:(c + 1) * LANES] for c in range((qw + kvw) // LANES)]
        sumsq = [_dot((t * t).astype(BF16), bd_ref[...]) for t in chunks]
        cb_ref[0, :rows, :] = _dot(h, w_ref[:, off:off + cw]).astype(BF16)
        off += cw
        cc = _dot(h, w_ref[:, off:off + cw])
        off += cw
        cu = _dot(h, w_ref[:, off:off + cw])
        off += cw
        s_ref[0, :rows, :] = (cc * cu).astype(BF16)
        ga_ref[0, :rows, :] = jax.nn.sigmoid(_dot(h, w_ref[:, off:off + d])).astype(BF16)
        off += d
        gc_ref[0, :rows, :] = jax.nn.sigmoid(_dot(h, w_ref[:, off:off + d])).astype(BF16)
        for c, (t, ss) in enumerate(zip(chunks, sumsq)):
            is_q = c < qw // LANES
            t = norm_rope(t, ss, (gq_ref if is_q else gk_ref)[...], cos, sin)
            if is_q:
                q_ref[0, :rows, c * LANES:(c + 1) * LANES] = (t * Q_SCALE).astype(BF16)
            else:
                ck = c - qw // LANES
                k_ref[0, :rows, ck * LANES:(ck + 1) * LANES] = t.astype(BF16)

    @pl.when(j < nreal)
    def _():
        body(x_ref[0], cos_ref[...], sin_ref[...], TM)
        for src, dst in zip(cast_in, cast_out):
            dst[...] = src[...].astype(BF16)

    @pl.when(j == nreal)
    def _():
        body(meta_ref[...], cos_ref[:BLK, :], sin_ref[:BLK, :], BLK)


def _cast_rows(arr, steps):
    rows, cols = arr.shape
    r = rows // steps
    ok = rows % steps == 0 and r % HALO == 0 and r * cols * 4 <= CAST_CHUNK_BYTES
    return r if ok else 0


def _stage_a(x, meta_blk, cos2, sin2, gmix, gq2, gk2, bd, w_in_b, cast_srcs=()):
    b, s, d = x.shape
    li = s + BLK
    nreal = s // TM
    qw, kvw, cw = N_HEADS * HEAD_DIM, N_KV_HEADS * HEAD_DIM, d
    const = lambda bi, j: (0, 0)
    rowblk = lambda bi, j: (bi, j, 0)
    out_w = (qw, kvw, kvw, cw, cw, d, d)
    chunk = lambda bi, j: (bi * nreal + jnp.minimum(j, nreal - 1), 0)
    cast_specs = [pl.BlockSpec((_cast_rows(a, b * nreal), a.shape[1]), chunk) for a in cast_srcs]
    outs = pl.pallas_call(
        functools.partial(_proj_kernel, nreal=nreal, d=d, qw=qw, kvw=kvw, cw=cw, n_cast=len(cast_srcs)),
        grid=(b, nreal + 1),
        in_specs=[
            pl.BlockSpec((1, TM, d), lambda bi, j: (bi, jnp.minimum(j, nreal - 1), 0)),
            pl.BlockSpec((BLK, d), const),
            pl.BlockSpec((TM, LANES), lambda bi, j: (j, 0)),
            pl.BlockSpec((TM, LANES), lambda bi, j: (j, 0)),
            pl.BlockSpec((1, d), const),
            pl.BlockSpec((1, LANES), const),
            pl.BlockSpec((1, LANES), const),
            pl.BlockSpec((LANES, LANES), const),
            pl.BlockSpec(w_in_b.shape, const, pipeline_mode=pl.Buffered(1)),
        ] + cast_specs,
        out_specs=[pl.BlockSpec((1, TM, w), rowblk) for w in out_w] + cast_specs,
        out_shape=[jax.ShapeDtypeStruct((b, li, w), BF16) for w in out_w]
        + [jax.ShapeDtypeStruct(a.shape, BF16) for a in cast_srcs],
        compiler_params=_cparams(2),
        name="proj_stage",
    )(x, meta_blk, cos2, sin2, gmix, gq2, gk2, bd, w_in_b, *cast_srcs)
    return outs[:7], outs[7:]


def _attn_kernel(q_ref, kp_ref, kc_ref, kn_ref, vp_ref, vc_ref, vn_ref, sink_ref, o_ref, *, nblk):
    sb = pl.program_id(1)
    kj = lax.broadcasted_iota(jnp.int32, (3 * BLK, BLK), 0)
    qi = lax.broadcasted_iota(jnp.int32, (3 * BLK, BLK), 1)
    band = (kj >= qi) & (kj <= qi + 2 * BLK)
    seg = jnp.right_shift(kj, 7)
    row = jnp.bitwise_and(kj, BLK - 1)
    lo_prev = jnp.where(sb == 0, META_ROW0, jnp.where(sb == nblk - 1, BLK, 0))
    lo_cur = jnp.where(sb == nblk - 1, META_ROW0, 0)
    lo_next = jnp.where(sb == nblk - 2, BLK, 0)
    lo = jnp.where(seg == 0, lo_prev, jnp.where(seg == 1, lo_cur, lo_next))
    ok = band & (row >= lo)
    lane = lax.broadcasted_iota(jnp.int32, (1, LANES), 1)
    first = lane < HEAD_DIM
    first_rows = lax.broadcasted_iota(jnp.int32, (LANES, 1), 0) < HEAD_DIM
    zero = jnp.zeros((), BF16)
    groups = N_HEADS // N_KV_HEADS
    ok2 = jnp.concatenate([ok, ok], axis=1)
    n_pairs = N_KV_HEADS * HEAD_DIM // LANES
    kcat, v_t = [], []
    for m in range(n_pairs):
        cols = slice(m * LANES, (m + 1) * LANES)
        kcat.append(jnp.concatenate([kp_ref[0, :, cols], kc_ref[0, :, cols], kn_ref[0, :, cols]], axis=0))
        vcat = jnp.concatenate([vp_ref[0, :, cols], vc_ref[0, :, cols], vn_ref[0, :, cols]], axis=0)
        v_t.append(jnp.concatenate([jnp.transpose(vcat.astype(F32)).astype(BF16),
                                    jnp.ones((DEN_ROWS, 3 * BLK), BF16)], axis=0))

    def scores(c):
        qc = q_ref[0, :, c * LANES:(c + 1) * LANES]
        lhs = jnp.concatenate([jnp.where(first, qc, zero), jnp.where(first, zero, qc)], axis=0)
        return _dot_nt(kcat[c // groups], lhs)

    n_chunks = n_pairs * groups
    pending = [scores(c) for c in range(ATTN_LOOKAHEAD)]
    for c in range(n_chunks):
        m, i = divmod(c, groups)
        if c + ATTN_LOOKAHEAD < n_chunks:
            pending.append(scores(c + ATTN_LOOKAHEAD))
        s_t = jnp.where(ok2, pending.pop(0), NEG)
        sink = sink_ref[m, :, 2 * i * BLK:(2 * i + 2) * BLK]
        mx = jnp.maximum(jnp.max(s_t, axis=0, keepdims=True), sink)
        p = jnp.exp2(s_t - mx)
        o_t = _dot(v_t[m], p.astype(BF16))
        den = o_t[LANES:LANES + 1, :] + jnp.exp2(sink - mx)
        o_t = o_t[:LANES, :] / den
        blk = jnp.where(first_rows, o_t[:, :BLK], o_t[:, BLK:])
        o_ref[0, :, c * LANES:(c + 1) * LANES] = jnp.transpose(blk).astype(BF16)


def _stage_b(q, k, v, sink_rows):
    b, li, qw = q.shape
    kvw = k.shape[-1]
    nblk = li // BLK
    prev = lambda bi, sb: (bi, (sb + nblk - 1) % nblk, 0)
    cur = lambda bi, sb: (bi, sb, 0)
    nxt = lambda bi, sb: (bi, (sb + 1) % nblk, 0)
    kv_specs = [pl.BlockSpec((1, BLK, kvw), f) for f in (prev, cur, nxt)]
    return pl.pallas_call(
        functools.partial(_attn_kernel, nblk=nblk),
        grid=(b, nblk),
        in_specs=[pl.BlockSpec((1, BLK, qw), cur)] + kv_specs + kv_specs
        + [pl.BlockSpec(sink_rows.shape, lambda bi, sb: (0, 0, 0))],
        out_specs=pl.BlockSpec((1, BLK, qw), cur),
        out_shape=jax.ShapeDtypeStruct((b, li, qw), BF16),
        compiler_params=_cparams(2),
        name="attn_stage",
    )(q, k, k, k, v, v, v, sink_rows)


def _mix_kernel(x_ref, meta_ref, attn_ref, cb_ref, s_ref, sp_ref, sn_ref, ga_ref, gc_ref,
                wconv_ref, gffn_ref, wa_ref, wc_ref, wo_ref, wr_ref,
                x1_ref, h2_ref, aff_ref, *, nreal):
    j = pl.program_id(1)

    def body(x, rows, n_pad):
        attn_p = _dot(attn_ref[0, :rows, :], wa_ref[...])
        s = s_ref[0, :rows, :].astype(F32)
        ridx = lax.broadcasted_iota(jnp.int32, (rows, 1), 0)
        s_prev = jnp.where(ridx == 0, sp_ref[0, HALO - 1:HALO, :].astype(F32), pltpu.roll(s, 1, 0))
        s_next = jnp.where(ridx == rows - 1, sn_ref[0, 0:1, :].astype(F32), pltpu.roll(s, rows - 1, 0))
        conv = wconv_ref[0:1, :] * s_prev + wconv_ref[1:2, :] * s + wconv_ref[2:3, :] * s_next
        conv_in = (cb_ref[0, :rows, :].astype(F32) * conv).astype(BF16)
        conv_p = _dot(conv_in, wc_ref[...])
        merged = ga_ref[0, :rows, :].astype(F32) * attn_p + gc_ref[0, :rows, :].astype(F32) * conv_p
        x1 = x + _dot(merged.astype(BF16), wo_ref[...])
        x1_ref[0, :rows, :] = x1
        ms = jnp.mean(x1 * x1, axis=-1, keepdims=True)
        h2 = (x1 * lax.rsqrt(ms + EPS) * gffn_ref[...]).astype(BF16)
        h2_ref[0, :rows, :] = h2
        logits = _dot_nt(wr_ref[...], h2)
        e = jnp.exp(logits - jnp.max(logits, axis=0, keepdims=True))
        aff = e / jnp.sum(e, axis=0, keepdims=True)
        if n_pad:
            aff = jnp.where(lax.broadcasted_iota(jnp.int32, aff.shape, 1) < n_pad, -1.0, aff)
        aff_ref[0, :, :rows] = aff

    @pl.when(j < nreal)
    def _():
        body(x_ref[0], TM, 0)

    @pl.when(j == nreal)
    def _():
        body(meta_ref[...], BLK, META_ROW0)


def _stage_c(x, meta_blk, attn, cb, sprod, ga, gc, wconv, gffn, wa, wc, wo, wr_t):
    b, s, d = x.shape
    li = s + BLK
    nreal = s // TM
    nhalo = li // HALO
    const = lambda bi, j: (0, 0)
    rowblk = lambda bi, j: (bi, j, 0)
    halo_prev = lambda bi, j: (bi, (j * (TM // HALO) + nhalo - 1) % nhalo, 0)
    halo_next = lambda bi, j: (bi, jnp.where(j == nreal, 0, (j + 1) * (TM // HALO)), 0)
    n_exp = wr_t.shape[0]
    wspec = lambda w: pl.BlockSpec(w.shape, const, pipeline_mode=pl.Buffered(1))
    return pl.pallas_call(
        functools.partial(_mix_kernel, nreal=nreal),
        grid=(b, nreal + 1),
        in_specs=[
            pl.BlockSpec((1, TM, d), lambda bi, j: (bi, jnp.minimum(j, nreal - 1), 0)),
            pl.BlockSpec((BLK, d), const),
            pl.BlockSpec((1, TM, d), rowblk),
            pl.BlockSpec((1, TM, d), rowblk),
            pl.BlockSpec((1, TM, d), rowblk),
            pl.BlockSpec((1, HALO, d), halo_prev),
            pl.BlockSpec((1, HALO, d), halo_next),
            pl.BlockSpec((1, TM, d), rowblk),
            pl.BlockSpec((1, TM, d), rowblk),
            pl.BlockSpec(wconv.shape, const),
            pl.BlockSpec((1, d), const),
            wspec(wa), wspec(wc), wspec(wo), wspec(wr_t),
        ],
        out_specs=[
            pl.BlockSpec((1, TM, d), rowblk),
            pl.BlockSpec((1, TM, d), rowblk),
            pl.BlockSpec((1, n_exp, TM), lambda bi, j: (bi, 0, j)),
        ],
        out_shape=[
            jax.ShapeDtypeStruct((b, li, d), F32),
            jax.ShapeDtypeStruct((b, li, d), BF16),
            jax.ShapeDtypeStruct((b, n_exp, li), F32),
        ],
        compiler_params=_cparams(2),
        name="mix_stage",
    )(x, meta_blk, attn, cb, sprod, sprod, sprod, ga, gc, wconv, gffn, wa, wc, wo, wr_t)


def _expert_kernel(xs_ref, wg_ref, wu_ref, wd_ref, y_ref):
    x = xs_ref[0]
    g = _dot(x, wg_ref[0])
    u = _dot(x, wu_ref[0])
    he = (g * jax.nn.sigmoid(g) * u).astype(BF16)
    y_ref[0] = _dot(he, wd_ref[0]).astype(y_ref.dtype)


def _stage_d(xs, wg, wu, wd):
    e, cap_p, d = xs.shape
    de = wg.shape[-1]
    tile = lambda ei, t: (ei, t, 0)
    wmap = lambda ei, t: (ei, 0, 0)
    return pl.pallas_call(
        _expert_kernel,
        grid=(e, cap_p // TE),
        in_specs=[
            pl.BlockSpec((1, TE, d), tile),
            pl.BlockSpec((1, d, de), wmap),
            pl.BlockSpec((1, d, de), wmap),
            pl.BlockSpec((1, de, d), wmap),
        ],
        out_specs=pl.BlockSpec((1, TE, d), tile),
        out_shape=jax.ShapeDtypeStruct((e, cap_p, d), BF16),
        compiler_params=_cparams(2),
        name="expert_stage",
    )(xs, wg, wu, wd)


def _combine_kernel(start_ref, rounds_ref, x1_ref, slot_ref, gate_ref, ye_ref, o_ref, ybuf, sem, *,
                    n_exp, nsteps):
    lin = pl.program_id(0) * pl.num_programs(1) + pl.program_id(1)

    def chunk_start(step, e, rnd):
        return jnp.minimum(start_ref[step * n_exp + e] + rnd * CH, ye_ref.shape[1] - CH)

    def chunk_copy(step, e, rnd, buf):
        st = pl.multiple_of(chunk_start(step, e, rnd), HALO)
        return pltpu.make_async_copy(ye_ref.at[e, pl.ds(st, CH), :],
                                     ybuf.at[buf, pl.ds(e * CH, CH), :], sem.at[buf])

    def fetch(step, rnd, buf):
        for e in range(n_exp):
            chunk_copy(step, e, rnd, buf).start()

    def wait(step, rnd, buf):
        for e in range(n_exp):
            chunk_copy(step, e, rnd, buf).wait()

    buf = lin % 2

    @pl.when(lin == 0)
    def _():
        fetch(0, 0, 0)

    @pl.when(lin + 1 < nsteps)
    def _():
        fetch(lin + 1, 0, 1 - buf)

    slot = slot_ref[0]
    gate = gate_ref[0]
    lane_c = lax.broadcasted_iota(jnp.int32, (1, CH), 1)

    def gathered(rnd):
        perm = []
        for e in range(n_exp):
            col = slot[:, e:e + 1]
            mine = col >= start_ref[lin * n_exp + e] + rnd * CH
            hit = (col - chunk_start(lin, e, rnd) == lane_c) & mine
            perm.append(jnp.where(hit, gate[:, e:e + 1], 0.0).astype(BF16))
        return _dot(jnp.concatenate(perm, axis=1), ybuf[buf])

    wait(lin, 0, buf)
    o_ref[0] = x1_ref[0] + gathered(0)

    def extra(rnd, carry):
        fetch(lin, rnd, buf)
        wait(lin, rnd, buf)
        o_ref[0] += gathered(rnd)
        return carry

    lax.fori_loop(1, rounds_ref[lin], extra, 0)


def _stage_e(x1, slot, gate, ye, start_al, rounds, s):
    b, li, d = x1.shape
    n_exp = slot.shape[-1]
    ntile = s // TM
    tile = lambda bi, j, *_: (bi, j, 0)
    return pl.pallas_call(
        functools.partial(_combine_kernel, n_exp=n_exp, nsteps=b * ntile),
        grid_spec=pltpu.PrefetchScalarGridSpec(
            num_scalar_prefetch=2,
            grid=(b, ntile),
            in_specs=[
                pl.BlockSpec((1, TM, d), tile),
                pl.BlockSpec((1, TM, n_exp), tile),
                pl.BlockSpec((1, TM, n_exp), tile),
                pl.BlockSpec(memory_space=pl.ANY),
            ],
            out_specs=pl.BlockSpec((1, TM, d), tile),
            scratch_shapes=[pltpu.VMEM((2, n_exp * CH, d), BF16), pltpu.SemaphoreType.DMA((2,))],
        ),
        out_shape=jax.ShapeDtypeStruct((b, s, d), F32),
        compiler_params=_cparams(2),
        name="combine_stage",
    )(start_al, rounds, x1, slot, gate, ye)


def _q_head_perm():
    groups = N_HEADS // N_KV_HEADS
    cols = []
    for m in range(N_KV_HEADS // 2):
        for i in range(groups):
            for h in (2 * groups * m + i, 2 * groups * m + groups + i):
                cols.extend(range(h * HEAD_DIM, (h + 1) * HEAD_DIM))
    return np.asarray(cols, np.int32)


def _rope_tables(s):
    pos = jnp.concatenate([N_META + jnp.arange(s, dtype=F32), jnp.zeros((META_ROW0,), F32),
                           jnp.arange(N_META, dtype=F32)])
    inv = ROPE_THETA ** (-jnp.arange(0, HEAD_DIM, 2, dtype=F32) / HEAD_DIM)
    ang = pos[:, None] * inv[None, :]
    reps = LANES // (HEAD_DIM // 2)
    cos2 = jnp.tile(jnp.cos(ang), (1, reps))
    sgn = jnp.where((jnp.arange(LANES) % HEAD_DIM) < HEAD_DIM // 2, -1.0, 1.0).astype(F32)
    sin2 = jnp.tile(jnp.sin(ang), (1, reps)) * sgn[None, :]
    return cos2, sin2


def _sink_rows(sink):
    groups = N_HEADS // N_KV_HEADS
    heads = np.asarray([[2 * groups * m + groups * half + i
                         for i in range(groups) for half in range(2)]
                        for m in range(N_KV_HEADS // 2)], np.int32)
    rows = jnp.repeat(sink.astype(F32)[heads] * LOG2E, BLK, axis=1)
    return rows[:, None, :]


def _threshold_kernel(aff_ref, thr_ref, *, cap):
    bits = pltpu.bitcast(aff_ref[...], jnp.int32)

    def step(i, prefix):
        cand = prefix | jnp.left_shift(jnp.int32(1), 30 - i)
        count = jnp.sum((bits >= cand).astype(jnp.int32), axis=1, keepdims=True)
        return jnp.where(count >= cap, cand, prefix)

    prefix = lax.fori_loop(0, 31, step, jnp.zeros((bits.shape[0], 1), jnp.int32))
    thr_ref[...] = jnp.broadcast_to(pltpu.bitcast(prefix, F32), thr_ref.shape)


def _thresholds(aff, cap):
    n_exp = aff.shape[0]
    out = pl.pallas_call(
        functools.partial(_threshold_kernel, cap=cap),
        out_shape=jax.ShapeDtypeStruct((n_exp, LANES), F32),
        compiler_params=pltpu.CompilerParams(vmem_limit_bytes=VMEM_LIMIT),
        name="threshold_stage",
    )(aff)
    return out[:, 0]


def _route(aff_t, s):
    b, n_exp, li = aff_t.shape
    n = b * li
    n_tok = b * (s + N_META)
    cap = max(1, CAPACITY_FACTOR * n_tok // n_exp)
    cap_p = max(-(-cap // TE) * TE, CH)
    aff = jnp.transpose(aff_t, (1, 0, 2)).reshape(n_exp, n)
    thr = _thresholds(aff, cap)[:, None]
    above = aff > thr
    equal = aff == thr
    room = cap - jnp.sum(above, axis=1, dtype=jnp.int32, keepdims=True)
    eq_upto = jnp.cumsum(equal, axis=1, dtype=jnp.int32).reshape(n_exp, b, li)
    eq_real_end = eq_upto[:, :, s - 1:s]
    eq_before_seq = eq_upto[:, :, 0:1] - equal.reshape(n_exp, b, li)[:, :, 0:1]
    eq_real = eq_real_end - eq_before_seq
    eq_meta = eq_upto[:, :, li - 1:li] - eq_real_end
    is_real = (jnp.arange(li) < s)[None, None, :]
    rank = (eq_upto + jnp.where(is_real, eq_meta, -eq_real)).reshape(n_exp, n)
    sel = above | (equal & (rank <= room))

    upto = jnp.cumsum(sel, axis=1, dtype=jnp.int32)
    before = upto - sel
    slot = jnp.transpose(jnp.where(sel, upto - 1, UNSELECTED)).reshape(b, li, n_exp)
    gate = jnp.transpose(jnp.where(sel, aff, 0.0)).reshape(b, li, n_exp)

    before_seq = before.reshape(n_exp, b, li)
    start = before_seq[:, :, 0:s:TM]
    end = before_seq[:, :, TM:s + 1:TM]
    start_al = start // HALO * HALO
    rounds = jnp.maximum(jnp.max(-(-(end - start_al) // CH), axis=0), 1)
    start_al = jnp.transpose(start_al, (1, 2, 0)).reshape(-1)

    ngrp = n // LANES
    grp_before = before[:, ::LANES]
    grp_end = jnp.concatenate([grp_before[:, 1:], upto[:, -1:]], axis=1)
    local = upto.reshape(n_exp, ngrp, LANES) - grp_before[:, :, None]
    gidx = jnp.arange(ngrp, dtype=jnp.int32)[None, :]
    byte = lambda v: jnp.stack([v // 256, v % 256], axis=-1)
    table = jnp.concatenate([local, byte(grp_before), jnp.broadcast_to(byte(gidx), (n_exp, ngrp, 2))],
                            axis=-1).astype(BF16)
    slots = jnp.arange(cap_p, dtype=jnp.int32)[None, :, None]
    in_grp = ((grp_before[:, None, :] <= slots) & (slots < grp_end[:, None, :])).astype(BF16)
    hit = jnp.einsum("esg,egc->esc", in_grp, table, preferred_element_type=F32).astype(jnp.int32)
    j = slots[:, :, 0] - (hit[:, :, LANES] * 256 + hit[:, :, LANES + 1])
    lane = jnp.sum(hit[:, :, :LANES] <= j[:, :, None], axis=-1, dtype=jnp.int32)
    rows = (hit[:, :, LANES + 2] * 256 + hit[:, :, LANES + 3]) * LANES + lane
    rows = jnp.where(slots[:, :, 0] < cap, rows, 0)
    return rows, slot, gate, start_al, rounds.reshape(-1)


def _project(x, meta_blk, prm, cast_srcs=()):
    b, s, d = x.shape
    assert s % TM == 0 and d % LANES == 0
    cos2, sin2 = _rope_tables(s)
    return _stage_a(x, meta_blk, cos2, sin2, prm["gmix"], prm["gq2"], prm["gk2"], prm["bd"],
                    prm["w_in"], cast_srcs)


def _encoder(x, meta_blk, prm, projected, expert_w):
    b, s, d = x.shape
    li = s + BLK
    q, k, v, cb, sprod, ga, gc = projected
    attn = _stage_b(q, k, v, prm["sink_rows"])
    x1, h2, aff_t = _stage_c(x, meta_blk, attn, cb, sprod, ga, gc, prm["wconv"], prm["gffn"],
                             prm["wa"], prm["wc"], prm["wo"], prm["wr_t"])
    rows, slot, gate, start_al, rounds = _route(aff_t, s)
    xs = h2.reshape(b * li, d)[rows]
    ye = _stage_d(xs, *expert_w)
    return _stage_e(x1, slot, gate, ye, start_al, rounds, s)


def kernel(x_prompt, x_sample, meta_tokens, g_mix, w_in, g_q, g_k, sink_logits, w_conv, w_attn_out,
           w_conv_out, w_out, g_ffn, w_router, w_expert_gate, w_expert_up, w_expert_down):
    assert w_in.shape[0] == 1, "single-layer block"
    d = x_prompt.shape[-1]
    qw = N_HEADS * HEAD_DIM
    perm = _q_head_perm()
    w_in0 = w_in[0]
    lane = np.arange(LANES)
    prm = {
        "w_in": jnp.concatenate([w_in0[:, :qw][:, perm], w_in0[:, qw:]], axis=1).astype(BF16),
        "gmix": g_mix[0][None, :].astype(F32),
        "gq2": jnp.tile(g_q[0], LANES // HEAD_DIM)[None, :].astype(F32),
        "gk2": jnp.tile(g_k[0], LANES // HEAD_DIM)[None, :].astype(F32),
        "bd": jnp.asarray((lane[:, None] // HEAD_DIM) == (lane[None, :] // HEAD_DIM), BF16),
        "sink_rows": _sink_rows(sink_logits[0]),
        "wconv": w_conv[0].astype(F32),
        "gffn": g_ffn[0][None, :].astype(F32),
        "wa": w_attn_out[0][perm, :].astype(BF16),
        "wc": w_conv_out[0].astype(BF16),
        "wo": w_out[0].astype(BF16),
        "wr_t": jnp.transpose(w_router[0]).astype(BF16),
    }
    meta_blk = jnp.concatenate([jnp.zeros((META_ROW0, d), F32), meta_tokens.astype(F32)], axis=0)
    expert_w = [w_expert_gate[0], w_expert_up[0], w_expert_down[0]]
    flat = [w.astype(F32).reshape(-1, w.shape[-1]) for w in expert_w]
    steps = x_prompt.shape[0] * (x_prompt.shape[1] // TM)
    if all(_cast_rows(w, steps) for w in flat):
        proj_prompt, cast = _project(x_prompt, meta_blk, prm, flat)
        expert_w = [c.reshape(w.shape) for c, w in zip(cast, expert_w)]
    else:
        proj_prompt, _ = _project(x_prompt, meta_blk, prm)
        expert_w = [w.astype(BF16) for w in expert_w]
    proj_sample, _ = _project(x_sample, meta_blk, prm)
    return (_encoder(x_prompt, meta_blk, prm, proj_prompt, expert_w),
            _encoder(x_sample, meta_blk, prm, proj_sample, expert_w))
```

```python
import functools

import numpy as np
import jax
import jax.numpy as jnp
from jax import lax
from jax.experimental import pallas as pl
from jax.experimental.pallas import tpu as pltpu

N_META = 16
N_HEADS = 16
N_KV_HEADS = 4
HEAD_DIM = 64
N_EXPERTS = 16
CAPACITY_FACTOR = 2
ROPE_THETA = 10000.0
EPS = 1e-6
NEG = -1e30

LANES = 128
BLK = 128
META_ROW0 = BLK - N_META
TM = 512
TE = 528
HALO = 16
DEN_ROWS = 16
ATTN_LOOKAHEAD = 4
ATTN_BLOCKS = 3
CH = 128
UNSELECTED = -(1 << 30)
NO_LANE = 1 << 30
LOG2E = 1.4426950408889634
Q_SCALE = HEAD_DIM ** -0.5 * LOG2E
VMEM_LIMIT = 56 * 1024 * 1024
CAST_CHUNK_BYTES = 2 * 1024 * 1024

F32 = jnp.float32
BF16 = jnp.bfloat16


def _dot(a, b):
    return jnp.dot(a, b, preferred_element_type=F32)


def _dot_nt(a, b):
    return lax.dot_general(a, b, (((1,), (1,)), ((), ())), preferred_element_type=F32)


def _cparams(n_axes):
    return pltpu.CompilerParams(dimension_semantics=("arbitrary",) * n_axes,
                                vmem_limit_bytes=VMEM_LIMIT)


def _proj_kernel(x_ref, meta_ref, cos_ref, sin_ref, gmix_ref, gq_ref, gk_ref, bd_ref, w_ref, *refs,
                 nreal, d, qw, kvw, cw, n_cast):
    cast_in, refs = refs[:n_cast], refs[n_cast:]
    (q_ref, k_ref, v_ref, cb_ref, s_ref, ga_ref, gc_ref), cast_out = refs[:7], refs[7:]
    j = pl.program_id(1)
    lane = lax.broadcasted_iota(jnp.int32, (1, LANES), 1)
    first_half = jnp.bitwise_and(lane, HEAD_DIM - 1) < (HEAD_DIM // 2)

    def norm_rope(t, ss, g, cos, sin):
        t = t * lax.rsqrt(ss * (1.0 / HEAD_DIM) + EPS) * g
        partner = jnp.where(first_half, pltpu.roll(t, LANES - HEAD_DIM // 2, 1),
                            pltpu.roll(t, HEAD_DIM // 2, 1))
        return t * cos + partner * sin

    def body(x, cos, sin, rows):
        ms = jnp.mean(x * x, axis=-1, keepdims=True)
        h = (x * lax.rsqrt(ms + EPS) * gmix_ref[...]).astype(BF16)
        qk = _dot(h, w_ref[:, :qw + kvw])
        off = qw + kvw
        v_ref[0, :rows, :] = _dot(h, w_ref[:, off:off + kvw]).astype(BF16)
        off += kvw
        chunks = [qk[:, c * LANES:(c + 1) * LANES] for c in range((qw + kvw) // LANES)]
        sumsq = [_dot((t * t).astype(BF16), bd_ref[...]) for t in chunks]
        cb_ref[0, :rows, :] = _dot(h, w_ref[:, off:off + cw]).astype(BF16)
        off += cw
        cc = _dot(h, w_ref[:, off:off + cw])
        off += cw
        cu = _dot(h, w_ref[:, off:off + cw])
        off += cw
        s_ref[0, :rows, :] = (cc * cu).astype(BF16)
        ga_ref[0, :rows, :] = jax.nn.sigmoid(_dot(h, w_ref[:, off:off + d])).astype(BF16)
        off += d
        gc_ref[0, :rows, :] = jax.nn.sigmoid(_dot(h, w_ref[:, off:off + d])).astype(BF16)
        for c, (t, ss) in enumerate(zip(chunks, sumsq)):
            is_q = c < qw // LANES
            t = norm_rope(t, ss, (gq_ref if is_q else gk_ref)[...], cos, sin)
            if is_q:
                q_ref[0, :rows, c * LANES:(c + 1) * LANES] = (t * Q_SCALE).astype(BF16)
            else:
                ck = c - qw // LANES
                k_ref[0, :rows, ck * LANES:(ck + 1) * LANES] = t.astype(BF16)

    @pl.when(j < nreal)
    def _():
        body(x_ref[0], cos_ref[...], sin_ref[...], TM)
        for src, dst in zip(cast_in, cast_out):
            dst[...] = src[...].astype(BF16)

    @pl.when(j == nreal)
    def _():
        body(meta_ref[...], cos_ref[:BLK, :], sin_ref[:BLK, :], BLK)


def _cast_rows(arr, steps):
    rows, cols = arr.shape
    r = rows // steps
    ok = rows % steps == 0 and r % HALO == 0 and r * cols * 4 <= CAST_CHUNK_BYTES
    return r if ok else 0


def _stage_a(x, meta_blk, cos2, sin2, gmix, gq2, gk2, bd, w_in_b, cast_srcs=()):
    b, s, d = x.shape
    li = s + BLK
    nreal = s // TM
    qw, kvw, cw = N_HEADS * HEAD_DIM, N_KV_HEADS * HEAD_DIM, d
    const = lambda bi, j: (0, 0)
    rowblk = lambda bi, j: (bi, j, 0)
    out_w = (qw, kvw, kvw, cw, cw, d, d)
    chunk = lambda bi, j: (bi * nreal + jnp.minimum(j, nreal - 1), 0)
    cast_specs = [pl.BlockSpec((_cast_rows(a, b * nreal), a.shape[1]), chunk) for a in cast_srcs]
    outs = pl.pallas_call(
        functools.partial(_proj_kernel, nreal=nreal, d=d, qw=qw, kvw=kvw, cw=cw, n_cast=len(cast_srcs)),
        grid=(b, nreal + 1),
        in_specs=[
            pl.BlockSpec((1, TM, d), lambda bi, j: (bi, jnp.minimum(j, nreal - 1), 0)),
            pl.BlockSpec((BLK, d), const),
            pl.BlockSpec((TM, LANES), lambda bi, j: (j, 0)),
            pl.BlockSpec((TM, LANES), lambda bi, j: (j, 0)),
            pl.BlockSpec((1, d), const),
            pl.BlockSpec((1, LANES), const),
            pl.BlockSpec((1, LANES), const),
            pl.BlockSpec((LANES, LANES), const),
            pl.BlockSpec(w_in_b.shape, const, pipeline_mode=pl.Buffered(1)),
        ] + cast_specs,
        out_specs=[pl.BlockSpec((1, TM, w), rowblk) for w in out_w] + cast_specs,
        out_shape=[jax.ShapeDtypeStruct((b, li, w), BF16) for w in out_w]
        + [jax.ShapeDtypeStruct(a.shape, BF16) for a in cast_srcs],
        compiler_params=_cparams(2),
        name="proj_stage",
    )(x, meta_blk, cos2, sin2, gmix, gq2, gk2, bd, w_in_b, *cast_srcs)
    return outs[:7], outs[7:]


def _attn_kernel(*refs, nblk):
    nq, nkv = ATTN_BLOCKS, ATTN_BLOCKS + 2
    q_refs, k_refs, v_refs = refs[:nq], refs[nq:nq + nkv], refs[nq + nkv:nq + 2 * nkv]
    sink_ref, o_ref = refs[nq + 2 * nkv:]
    kj = lax.broadcasted_iota(jnp.int32, (3 * BLK, BLK), 0)
    qi = lax.broadcasted_iota(jnp.int32, (3 * BLK, BLK), 1)
    band = (kj >= qi) & (kj <= qi + 2 * BLK)
    seg = jnp.right_shift(kj, 7)
    row = jnp.bitwise_and(kj, BLK - 1)
    lane = lax.broadcasted_iota(jnp.int32, (1, LANES), 1)
    first = lane < HEAD_DIM
    first_rows = lax.broadcasted_iota(jnp.int32, (LANES, 1), 0) < HEAD_DIM
    zero = jnp.zeros((), BF16)
    groups = N_HEADS // N_KV_HEADS
    n_pairs = N_KV_HEADS * HEAD_DIM // LANES
    ok2, kcat, v_t = [], [], []
    for u in range(nq):
        sb = pl.program_id(1) * nq + u
        lo_prev = jnp.where(sb == 0, META_ROW0, jnp.where(sb == nblk - 1, BLK, 0))
        lo_cur = jnp.where(sb == nblk - 1, META_ROW0, 0)
        lo_next = jnp.where(sb == nblk - 2, BLK, 0)
        lo = jnp.where(seg == 0, lo_prev, jnp.where(seg == 1, lo_cur, lo_next))
        ok = band & (row >= lo)
        ok2.append(jnp.concatenate([ok, ok], axis=1))
        for m in range(n_pairs):
            cols = slice(m * LANES, (m + 1) * LANES)
            kcat.append(jnp.concatenate([r[0, :, cols] for r in k_refs[u:u + 3]], axis=0))
            vcat = jnp.concatenate([r[0, :, cols] for r in v_refs[u:u + 3]], axis=0)
            v_t.append(jnp.concatenate([jnp.transpose(vcat.astype(F32)).astype(BF16),
                                        jnp.ones((DEN_ROWS, 3 * BLK), BF16)], axis=0))

    per_blk = n_pairs * groups

    def scores(n):
        u, c = divmod(n, per_blk)
        qc = q_refs[u][0, :, c * LANES:(c + 1) * LANES]
        lhs = jnp.concatenate([jnp.where(first, qc, zero), jnp.where(first, zero, qc)], axis=0)
        return _dot_nt(kcat[u * n_pairs + c // groups], lhs)

    n_chunks = nq * per_blk
    pending = [scores(n) for n in range(ATTN_LOOKAHEAD)]
    for n in range(n_chunks):
        u, c = divmod(n, per_blk)
        m, i = divmod(c, groups)
        if n + ATTN_LOOKAHEAD < n_chunks:
            pending.append(scores(n + ATTN_LOOKAHEAD))
        s_t = jnp.where(ok2[u], pending.pop(0), NEG)
        sink = sink_ref[m, :, 2 * i * BLK:(2 * i + 2) * BLK]
        mx = jnp.maximum(jnp.max(s_t, axis=0, keepdims=True), sink)
        p = jnp.exp2(s_t - mx)
        o_t = _dot(v_t[u * n_pairs + m], p.astype(BF16))
        den = o_t[LANES:LANES + 1, :] + jnp.exp2(sink - mx)
        o_t = o_t[:LANES, :] / den
        blk = jnp.where(first_rows, o_t[:, :BLK], o_t[:, BLK:])
        o_ref[0, u * BLK:(u + 1) * BLK, c * LANES:(c + 1) * LANES] = jnp.transpose(blk).astype(BF16)


def _stage_b(q, k, v, sink_rows):
    b, li, qw = q.shape
    kvw = k.shape[-1]
    nblk = li // BLK
    nq = ATTN_BLOCKS
    q_specs = [pl.BlockSpec((1, BLK, qw), lambda bi, j, u=u: (bi, jnp.minimum(j * nq + u, nblk - 1), 0))
               for u in range(nq)]
    kv_specs = [pl.BlockSpec((1, BLK, kvw), lambda bi, j, u=u: (bi, (j * nq + u + nblk - 1) % nblk, 0))
                for u in range(nq + 2)]
    return pl.pallas_call(
        functools.partial(_attn_kernel, nblk=nblk),
        grid=(b, -(-nblk // nq)),
        in_specs=q_specs + kv_specs + kv_specs
        + [pl.BlockSpec(sink_rows.shape, lambda bi, j: (0, 0, 0))],
        out_specs=pl.BlockSpec((1, nq * BLK, qw), lambda bi, j: (bi, j, 0)),
        out_shape=jax.ShapeDtypeStruct((b, li, qw), BF16),
        compiler_params=_cparams(2),
        name="attn_stage",
    )(*([q] * nq + [k] * (nq + 2) + [v] * (nq + 2) + [sink_rows]))


def _mix_kernel(x_ref, meta_ref, attn_ref, cb_ref, s_ref, sp_ref, sn_ref, ga_ref, gc_ref,
                wconv_ref, gffn_ref, wa_ref, wc_ref, wo_ref, wr_ref,
                x1_ref, h2_ref, aff_ref, *, nreal):
    j = pl.program_id(1)

    def body(x, rows, n_pad):
        attn_p = _dot(attn_ref[0, :rows, :], wa_ref[...])
        s = s_ref[0, :rows, :].astype(F32)
        ridx = lax.broadcasted_iota(jnp.int32, (rows, 1), 0)
        s_prev = jnp.where(ridx == 0, sp_ref[0, HALO - 1:HALO, :].astype(F32), pltpu.roll(s, 1, 0))
        s_next = jnp.where(ridx == rows - 1, sn_ref[0, 0:1, :].astype(F32), pltpu.roll(s, rows - 1, 0))
        conv = wconv_ref[0:1, :] * s_prev + wconv_ref[1:2, :] * s + wconv_ref[2:3, :] * s_next
        conv_in = (cb_ref[0, :rows, :].astype(F32) * conv).astype(BF16)
        conv_p = _dot(conv_in, wc_ref[...])
        merged = ga_ref[0, :rows, :].astype(F32) * attn_p + gc_ref[0, :rows, :].astype(F32) * conv_p
        x1 = x + _dot(merged.astype(BF16), wo_ref[...])
        x1_ref[0, :rows, :] = x1
        ms = jnp.mean(x1 * x1, axis=-1, keepdims=True)
        h2 = (x1 * lax.rsqrt(ms + EPS) * gffn_ref[...]).astype(BF16)
        h2_ref[0, :rows, :] = h2
        logits = _dot_nt(wr_ref[...], h2)
        e = jnp.exp(logits - jnp.max(logits, axis=0, keepdims=True))
        aff = e / jnp.sum(e, axis=0, keepdims=True)
        if n_pad:
            aff = jnp.where(lax.broadcasted_iota(jnp.int32, aff.shape, 1) < n_pad, -1.0, aff)
        aff_ref[0, :, :rows] = aff

    @pl.when(j < nreal)
    def _():
        body(x_ref[0], TM, 0)

    @pl.when(j == nreal)
    def _():
        body(meta_ref[...], BLK, META_ROW0)


def _stage_c(x, meta_blk, attn, cb, sprod, ga, gc, wconv, gffn, wa, wc, wo, wr_t):
    b, s, d = x.shape
    li = s + BLK
    nreal = s // TM
    nhalo = li // HALO
    const = lambda bi, j: (0, 0)
    rowblk = lambda bi, j: (bi, j, 0)
    halo_prev = lambda bi, j: (bi, (j * (TM // HALO) + nhalo - 1) % nhalo, 0)
    halo_next = lambda bi, j: (bi, jnp.where(j == nreal, 0, (j + 1) * (TM // HALO)), 0)
    n_exp = wr_t.shape[0]
    wspec = lambda w: pl.BlockSpec(w.shape, const, pipeline_mode=pl.Buffered(1))
    return pl.pallas_call(
        functools.partial(_mix_kernel, nreal=nreal),
        grid=(b, nreal + 1),
        in_specs=[
            pl.BlockSpec((1, TM, d), lambda bi, j: (bi, jnp.minimum(j, nreal - 1), 0)),
            pl.BlockSpec((BLK, d), const),
            pl.BlockSpec((1, TM, d), rowblk),
            pl.BlockSpec((1, TM, d), rowblk),
            pl.BlockSpec((1, TM, d), rowblk),
            pl.BlockSpec((1, HALO, d), halo_prev),
            pl.BlockSpec((1, HALO, d), halo_next),
            pl.BlockSpec((1, TM, d), rowblk),
            pl.BlockSpec((1, TM, d), rowblk),
            pl.BlockSpec(wconv.shape, const),
            pl.BlockSpec((1, d), const),
            wspec(wa), wspec(wc), wspec(wo), wspec(wr_t),
        ],
        out_specs=[
            pl.BlockSpec((1, TM, d), rowblk),
            pl.BlockSpec((1, TM, d), rowblk),
            pl.BlockSpec((1, n_exp, TM), lambda bi, j: (bi, 0, j)),
        ],
        out_shape=[
            jax.ShapeDtypeStruct((b, li, d), F32),
            jax.ShapeDtypeStruct((b, li, d), BF16),
            jax.ShapeDtypeStruct((b, n_exp, li), F32),
        ],
        compiler_params=_cparams(2),
        name="mix_stage",
    )(x, meta_blk, attn, cb, sprod, sprod, sprod, ga, gc, wconv, gffn, wa, wc, wo, wr_t)


def _expert_kernel(xs_ref, wg_ref, wu_ref, wd_ref, y_ref):
    x = xs_ref[0]
    g = _dot(x, wg_ref[0])
    u = _dot(x, wu_ref[0])
    he = (g * jax.nn.sigmoid(g) * u).astype(BF16)
    y_ref[0] = _dot(he, wd_ref[0]).astype(y_ref.dtype)


def _stage_d(xs, wg, wu, wd):
    e, cap_p, d = xs.shape
    de = wg.shape[-1]
    tile = lambda ei, t: (ei, t, 0)
    wmap = lambda ei, t: (ei, 0, 0)
    return pl.pallas_call(
        _expert_kernel,
        grid=(e, cap_p // TE),
        in_specs=[
            pl.BlockSpec((1, TE, d), tile),
            pl.BlockSpec((1, d, de), wmap),
            pl.BlockSpec((1, d, de), wmap),
            pl.BlockSpec((1, de, d), wmap),
        ],
        out_specs=pl.BlockSpec((1, TE, d), tile),
        out_shape=jax.ShapeDtypeStruct((e, cap_p, d), BF16),
        compiler_params=_cparams(2),
        name="expert_stage",
    )(xs, wg, wu, wd)


def _combine_kernel(start_ref, rounds_ref, x1_ref, slot_ref, gate_ref, ye_ref, o_ref, ybuf, sem, *,
                    n_exp, nsteps):
    lin = pl.program_id(0) * pl.num_programs(1) + pl.program_id(1)

    def chunk_start(step, e, rnd):
        return jnp.minimum(start_ref[step * n_exp + e] + rnd * CH, ye_ref.shape[1] - CH)

    def chunk_copy(step, e, rnd, buf):
        st = pl.multiple_of(chunk_start(step, e, rnd), HALO)
        return pltpu.make_async_copy(ye_ref.at[e, pl.ds(st, CH), :],
                                     ybuf.at[buf, pl.ds(e * CH, CH), :], sem.at[buf])

    def fetch(step, rnd, buf):
        for e in range(n_exp):
            chunk_copy(step, e, rnd, buf).start()

    def wait(step, rnd, buf):
        for e in range(n_exp):
            chunk_copy(step, e, rnd, buf).wait()

    buf = lin % 2

    @pl.when(lin == 0)
    def _():
        fetch(0, 0, 0)

    @pl.when(lin + 1 < nsteps)
    def _():
        fetch(lin + 1, 0, 1 - buf)

    slot = slot_ref[0]
    gate = gate_ref[0]
    lane_c = lax.broadcasted_iota(jnp.int32, (1, CH), 1)

    def gathered(rnd):
        perm = []
        for e in range(n_exp):
            st = chunk_start(lin, e, rnd)
            first_lane = start_ref[lin * n_exp + e] + rnd * CH - st
            lanes = jnp.where(lane_c >= first_lane, lane_c, NO_LANE)
            hit = slot[:, e:e + 1] - st == lanes
            perm.append(jnp.where(hit, gate[:, e:e + 1], 0.0).astype(BF16))
        return _dot(jnp.concatenate(perm, axis=1), ybuf[buf])

    wait(lin, 0, buf)
    o_ref[0] = x1_ref[0] + gathered(0)

    def extra(rnd, carry):
        fetch(lin, rnd, buf)
        wait(lin, rnd, buf)
        o_ref[0] += gathered(rnd)
        return carry

    lax.fori_loop(1, rounds_ref[lin], extra, 0)


def _stage_e(x1, slot, gate, ye, start_al, rounds, s):
    b, li, d = x1.shape
    n_exp = slot.shape[-1]
    ntile = s // TM
    tile = lambda bi, j, *_: (bi, j, 0)
    return pl.pallas_call(
        functools.partial(_combine_kernel, n_exp=n_exp, nsteps=b * ntile),
        grid_spec=pltpu.PrefetchScalarGridSpec(
            num_scalar_prefetch=2,
            grid=(b, ntile),
            in_specs=[
                pl.BlockSpec((1, TM, d), tile),
                pl.BlockSpec((1, TM, n_exp), tile),
                pl.BlockSpec((1, TM, n_exp), tile),
                pl.BlockSpec(memory_space=pl.ANY),
            ],
            out_specs=pl.BlockSpec((1, TM, d), tile),
            scratch_shapes=[pltpu.VMEM((2, n_exp * CH, d), BF16), pltpu.SemaphoreType.DMA((2,))],
        ),
        out_shape=jax.ShapeDtypeStruct((b, s, d), F32),
        compiler_params=_cparams(2),
        name="combine_stage",
    )(start_al, rounds, x1, slot, gate, ye)


def _q_head_perm():
    groups = N_HEADS // N_KV_HEADS
    cols = []
    for m in range(N_KV_HEADS // 2):
        for i in range(groups):
            for h in (2 * groups * m + i, 2 * groups * m + groups + i):
                cols.extend(range(h * HEAD_DIM, (h + 1) * HEAD_DIM))
    return np.asarray(cols, np.int32)


def _rope_tables(s):
    pos = jnp.concatenate([N_META + jnp.arange(s, dtype=F32), jnp.zeros((META_ROW0,), F32),
                           jnp.arange(N_META, dtype=F32)])
    inv = ROPE_THETA ** (-jnp.arange(0, HEAD_DIM, 2, dtype=F32) / HEAD_DIM)
    ang = pos[:, None] * inv[None, :]
    reps = LANES // (HEAD_DIM // 2)
    cos2 = jnp.tile(jnp.cos(ang), (1, reps))
    sgn = jnp.where((jnp.arange(LANES) % HEAD_DIM) < HEAD_DIM // 2, -1.0, 1.0).astype(F32)
    sin2 = jnp.tile(jnp.sin(ang), (1, reps)) * sgn[None, :]
    return cos2, sin2


def _sink_rows(sink):
    groups = N_HEADS // N_KV_HEADS
    heads = np.asarray([[2 * groups * m + groups * half + i
                         for i in range(groups) for half in range(2)]
                        for m in range(N_KV_HEADS // 2)], np.int32)
    rows = jnp.repeat(sink.astype(F32)[heads] * LOG2E, BLK, axis=1)
    return rows[:, None, :]


def _threshold_kernel(aff_ref, thr_ref, *, cap):
    bits = pltpu.bitcast(aff_ref[...], jnp.int32)

    def step(i, prefix):
        cand = prefix | jnp.left_shift(jnp.int32(1), 30 - i)
        count = jnp.sum((bits >= cand).astype(jnp.int32), axis=1, keepdims=True)
        return jnp.where(count >= cap, cand, prefix)

    prefix = lax.fori_loop(0, 31, step, jnp.zeros((bits.shape[0], 1), jnp.int32))
    thr_ref[...] = jnp.broadcast_to(pltpu.bitcast(prefix, F32), thr_ref.shape)


def _thresholds(aff, cap):
    n_exp = aff.shape[0]
    out = pl.pallas_call(
        functools.partial(_threshold_kernel, cap=cap),
        out_shape=jax.ShapeDtypeStruct((n_exp, LANES), F32),
        compiler_params=pltpu.CompilerParams(vmem_limit_bytes=VMEM_LIMIT),
        name="threshold_stage",
    )(aff)
    return out[:, 0]


def _route(aff_t, s):
    b, n_exp, li = aff_t.shape
    n = b * li
    n_tok = b * (s + N_META)
    cap = max(1, CAPACITY_FACTOR * n_tok // n_exp)
    cap_p = max(-(-cap // TE) * TE, CH)
    aff = jnp.transpose(aff_t, (1, 0, 2)).reshape(n_exp, n)
    thr = _thresholds(aff, cap)[:, None]
    above = aff > thr
    equal = aff == thr
    room = cap - jnp.sum(above, axis=1, dtype=jnp.int32, keepdims=True)
    eq_upto = jnp.cumsum(equal, axis=1, dtype=jnp.int32).reshape(n_exp, b, li)
    eq_real_end = eq_upto[:, :, s - 1:s]
    eq_before_seq = eq_upto[:, :, 0:1] - equal.reshape(n_exp, b, li)[:, :, 0:1]
    eq_real = eq_real_end - eq_before_seq
    eq_meta = eq_upto[:, :, li - 1:li] - eq_real_end
    is_real = (jnp.arange(li) < s)[None, None, :]
    rank = (eq_upto + jnp.where(is_real, eq_meta, -eq_real)).reshape(n_exp, n)
    sel = above | (equal & (rank <= room))

    upto = jnp.cumsum(sel, axis=1, dtype=jnp.int32)
    before = upto - sel
    slot = jnp.transpose(jnp.where(sel, upto - 1, UNSELECTED)).reshape(b, li, n_exp)
    gate = jnp.transpose(jnp.where(sel, aff, 0.0)).reshape(b, li, n_exp)

    before_seq = before.reshape(n_exp, b, li)
    start = before_seq[:, :, 0:s:TM]
    end = before_seq[:, :, TM:s + 1:TM]
    start_al = start // HALO * HALO
    rounds = jnp.maximum(jnp.max(-(-(end - start_al) // CH), axis=0), 1)
    start_al = jnp.transpose(start_al, (1, 2, 0)).reshape(-1)

    ngrp = n // LANES
    grp_before = before[:, ::LANES]
    grp_end = jnp.concatenate([grp_before[:, 1:], upto[:, -1:]], axis=1)
    local = upto.reshape(n_exp, ngrp, LANES) - grp_before[:, :, None]
    gidx = jnp.arange(ngrp, dtype=jnp.int32)[None, :]
    byte = lambda v: jnp.stack([v // 256, v % 256], axis=-1)
    table = jnp.concatenate([local, byte(grp_before), jnp.broadcast_to(byte(gidx), (n_exp, ngrp, 2))],
                            axis=-1).astype(BF16)
    slots = jnp.arange(cap_p, dtype=jnp.int32)[None, :, None]
    in_grp = ((grp_before[:, None, :] <= slots) & (slots < grp_end[:, None, :])).astype(BF16)
    hit = jnp.einsum("esg,egc->esc", in_grp, table, preferred_element_type=F32).astype(jnp.int32)
    j = slots[:, :, 0] - (hit[:, :, LANES] * 256 + hit[:, :, LANES + 1])
    lane = jnp.sum(hit[:, :, :LANES] <= j[:, :, None], axis=-1, dtype=jnp.int32)
    rows = (hit[:, :, LANES + 2] * 256 + hit[:, :, LANES + 3]) * LANES + lane
    rows = jnp.where(slots[:, :, 0] < cap, rows, 0)
    return rows, slot, gate, start_al, rounds.reshape(-1)


def _project(x, meta_blk, prm, cast_srcs=()):
    b, s, d = x.shape
    assert s % TM == 0 and d % LANES == 0
    cos2, sin2 = _rope_tables(s)
    return _stage_a(x, meta_blk, cos2, sin2, prm["gmix"], prm["gq2"], prm["gk2"], prm["bd"],
                    prm["w_in"], cast_srcs)


def _encoder(x, meta_blk, prm, projected, expert_w):
    b, s, d = x.shape
    li = s + BLK
    q, k, v, cb, sprod, ga, gc = projected
    attn = _stage_b(q, k, v, prm["sink_rows"])
    x1, h2, aff_t = _stage_c(x, meta_blk, attn, cb, sprod, ga, gc, prm["wconv"], prm["gffn"],
                             prm["wa"], prm["wc"], prm["wo"], prm["wr_t"])
    rows, slot, gate, start_al, rounds = _route(aff_t, s)
    xs = h2.reshape(b * li, d)[rows]
    ye = _stage_d(xs, *expert_w)
    return _stage_e(x1, slot, gate, ye, start_al, rounds, s)


def kernel(x_prompt, x_sample, meta_tokens, g_mix, w_in, g_q, g_k, sink_logits, w_conv, w_attn_out,
           w_conv_out, w_out, g_ffn, w_router, w_expert_gate, w_expert_up, w_expert_down):
    assert w_in.shape[0] == 1, "single-layer block"
    d = x_prompt.shape[-1]
    qw = N_HEADS * HEAD_DIM
    perm = _q_head_perm()
    w_in0 = w_in[0]
    lane = np.arange(LANES)
    prm = {
        "w_in": jnp.concatenate([w_in0[:, :qw][:, perm], w_in0[:, qw:]], axis=1).astype(BF16),
        "gmix": g_mix[0][None, :].astype(F32),
        "gq2": jnp.tile(g_q[0], LANES // HEAD_DIM)[None, :].astype(F32),
        "gk2": jnp.tile(g_k[0], LANES // HEAD_DIM)[None, :].astype(F32),
        "bd": jnp.asarray((lane[:, None] // HEAD_DIM) == (lane[None, :] // HEAD_DIM), BF16),
        "sink_rows": _sink_rows(sink_logits[0]),
        "wconv": w_conv[0].astype(F32),
        "gffn": g_ffn[0][None, :].astype(F32),
        "wa": w_attn_out[0][perm, :].astype(BF16),
        "wc": w_conv_out[0].astype(BF16),
        "wo": w_out[0].astype(BF16),
        "wr_t": jnp.transpose(w_router[0]).astype(BF16),
    }
    meta_blk = jnp.concatenate([jnp.zeros((META_ROW0, d), F32), meta_tokens.astype(F32)], axis=0)
    expert_w = [w_expert_gate[0], w_expert_up[0], w_expert_down[0]]
    flat = [w.astype(F32).reshape(-1, w.shape[-1]) for w in expert_w]
    steps = x_prompt.shape[0] * (x_prompt.shape[1] // TM)
    if all(_cast_rows(w, steps) for w in flat):
        proj_prompt, cast = _project(x_prompt, meta_blk, prm, flat)
        expert_w = [c.reshape(w.shape) for c, w in zip(cast, expert_w)]
    else:
        proj_prompt, _ = _project(x_prompt, meta_blk, prm)
        expert_w = [w.astype(BF16) for w in expert_w]
    proj_sample, _ = _project(x_sample, meta_blk, prm)
    return (_encoder(x_prompt, meta_blk, prm, proj_prompt, expert_w),
            _encoder(x_sample, meta_blk, prm, proj_sample, expert_w))
```

```python
import functools

import numpy as np
import jax
import jax.numpy as jnp
from jax import lax
from jax.experimental import pallas as pl
from jax.experimental.pallas import tpu as pltpu

N_META = 16
N_HEADS = 16
N_KV_HEADS = 4
HEAD_DIM = 64
N_EXPERTS = 16
CAPACITY_FACTOR = 2
ROPE_THETA = 10000.0
EPS = 1e-6
NEG = -1e30

LANES = 128
BLK = 128
META_ROW0 = BLK - N_META
TM = 512
TE = 528
HALO = 16
DEN_ROWS = 16
ATTN_LOOKAHEAD = 4
ATTN_BLOCKS = 3
CH = 128
UNSELECTED = -(1 << 30)
NO_LANE = 1 << 30
LOG2E = 1.4426950408889634
Q_SCALE = HEAD_DIM ** -0.5 * LOG2E
VMEM_LIMIT = 56 * 1024 * 1024
CAST_CHUNK_BYTES = 2 * 1024 * 1024

F32 = jnp.float32
BF16 = jnp.bfloat16


def _dot(a, b):
    return jnp.dot(a, b, preferred_element_type=F32)


def _dot_nt(a, b):
    return lax.dot_general(a, b, (((1,), (1,)), ((), ())), preferred_element_type=F32)


def _cparams(n_axes):
    return pltpu.CompilerParams(dimension_semantics=("arbitrary",) * n_axes,
                                vmem_limit_bytes=VMEM_LIMIT)


def _proj_kernel(x_ref, meta_ref, cos_ref, sin_ref, gmix_ref, gq_ref, gk_ref, bd_ref, w_ref, *refs,
                 nreal, d, qw, kvw, cw, n_cast):
    cast_in, refs = refs[:n_cast], refs[n_cast:]
    (q_ref, k_ref, v_ref, cb_ref, s_ref, ga_ref, gc_ref), cast_out = refs[:7], refs[7:]
    j = pl.program_id(1)
    lane = lax.broadcasted_iota(jnp.int32, (1, LANES), 1)
    first_half = jnp.bitwise_and(lane, HEAD_DIM - 1) < (HEAD_DIM // 2)

    def norm_rope(t, ss, g, cos, sin):
        t = t * lax.rsqrt(ss * (1.0 / HEAD_DIM) + EPS) * g
        partner = jnp.where(first_half, pltpu.roll(t, LANES - HEAD_DIM // 2, 1),
                            pltpu.roll(t, HEAD_DIM // 2, 1))
        return t * cos + partner * sin

    def body(x, cos, sin, rows):
        ms = jnp.mean(x * x, axis=-1, keepdims=True)
        h = (x * lax.rsqrt(ms + EPS) * gmix_ref[...]).astype(BF16)
        qk = _dot(h, w_ref[:, :qw + kvw])
        off = qw + kvw
        v_ref[0, :rows, :] = _dot(h, w_ref[:, off:off + kvw]).astype(BF16)
        off += kvw
        chunks = [qk[:, c * LANES:(c + 1) * LANES] for c in range((qw + kvw) // LANES)]
        sumsq = [_dot((t * t).astype(BF16), bd_ref[...]) for t in chunks]
        cb_ref[0, :rows, :] = _dot(h, w_ref[:, off:off + cw]).astype(BF16)
        off += cw
        cc = _dot(h, w_ref[:, off:off + cw])
        off += cw
        cu = _dot(h, w_ref[:, off:off + cw])
        off += cw
        s_ref[0, :rows, :] = (cc * cu).astype(BF16)
        ga_ref[0, :rows, :] = jax.nn.sigmoid(_dot(h, w_ref[:, off:off + d])).astype(BF16)
        off += d
        gc_ref[0, :rows, :] = jax.nn.sigmoid(_dot(h, w_ref[:, off:off + d])).astype(BF16)
        for c, (t, ss) in enumerate(zip(chunks, sumsq)):
            is_q = c < qw // LANES
            t = norm_rope(t, ss, (gq_ref if is_q else gk_ref)[...], cos, sin)
            if is_q:
                q_ref[0, :rows, c * LANES:(c + 1) * LANES] = (t * Q_SCALE).astype(BF16)
            else:
                ck = c - qw // LANES
                k_ref[0, :rows, ck * LANES:(ck + 1) * LANES] = t.astype(BF16)

    @pl.when(j < nreal)
    def _():
        body(x_ref[0], cos_ref[...], sin_ref[...], TM)
        for src, dst in zip(cast_in, cast_out):
            dst[...] = src[...].astype(BF16)

    @pl.when(j == nreal)
    def _():
        body(meta_ref[...], cos_ref[:BLK, :], sin_ref[:BLK, :], BLK)


def _cast_rows(arr, steps):
    rows, cols = arr.shape
    r = rows // steps
    ok = rows % steps == 0 and r % HALO == 0 and r * cols * 4 <= CAST_CHUNK_BYTES
    return r if ok else 0


def _stage_a(x, meta_blk, cos2, sin2, gmix, gq2, gk2, bd, w_in_b, cast_srcs=()):
    b, s, d = x.shape
    li = s + BLK
    nreal = s // TM
    qw, kvw, cw = N_HEADS * HEAD_DIM, N_KV_HEADS * HEAD_DIM, d
    const = lambda bi, j: (0, 0)
    rowblk = lambda bi, j: (bi, j, 0)
    out_w = (qw, kvw, kvw, cw, cw, d, d)
    chunk = lambda bi, j: (bi * nreal + jnp.minimum(j, nreal - 1), 0)
    cast_specs = [pl.BlockSpec((_cast_rows(a, b * nreal), a.shape[1]), chunk) for a in cast_srcs]
    outs = pl.pallas_call(
        functools.partial(_proj_kernel, nreal=nreal, d=d, qw=qw, kvw=kvw, cw=cw, n_cast=len(cast_srcs)),
        grid=(b, nreal + 1),
        in_specs=[
            pl.BlockSpec((1, TM, d), lambda bi, j: (bi, jnp.minimum(j, nreal - 1), 0)),
            pl.BlockSpec((BLK, d), const),
            pl.BlockSpec((TM, LANES), lambda bi, j: (j, 0)),
            pl.BlockSpec((TM, LANES), lambda bi, j: (j, 0)),
            pl.BlockSpec((1, d), const),
            pl.BlockSpec((1, LANES), const),
            pl.BlockSpec((1, LANES), const),
            pl.BlockSpec((LANES, LANES), const),
            pl.BlockSpec(w_in_b.shape, const, pipeline_mode=pl.Buffered(1)),
        ] + cast_specs,
        out_specs=[pl.BlockSpec((1, TM, w), rowblk) for w in out_w] + cast_specs,
        out_shape=[jax.ShapeDtypeStruct((b, li, w), BF16) for w in out_w]
        + [jax.ShapeDtypeStruct(a.shape, BF16) for a in cast_srcs],
        compiler_params=_cparams(2),
        name="proj_stage",
    )(x, meta_blk, cos2, sin2, gmix, gq2, gk2, bd, w_in_b, *cast_srcs)
    return outs[:7], outs[7:]


def _attn_kernel(*refs, nblk):
    nq, nkv = ATTN_BLOCKS, ATTN_BLOCKS + 2
    q_refs, k_refs, v_refs = refs[:nq], refs[nq:nq + nkv], refs[nq + nkv:nq + 2 * nkv]
    sink_ref, o_ref = refs[nq + 2 * nkv:]
    kj = lax.broadcasted_iota(jnp.int32, (3 * BLK, BLK), 0)
    qi = lax.broadcasted_iota(jnp.int32, (3 * BLK, BLK), 1)
    band = (kj >= qi) & (kj <= qi + 2 * BLK)
    seg = jnp.right_shift(kj, 7)
    row = jnp.bitwise_and(kj, BLK - 1)
    lane = lax.broadcasted_iota(jnp.int32, (1, LANES), 1)
    first = lane < HEAD_DIM
    first_rows = lax.broadcasted_iota(jnp.int32, (LANES, 1), 0) < HEAD_DIM
    zero = jnp.zeros((), BF16)
    groups = N_HEADS // N_KV_HEADS
    n_pairs = N_KV_HEADS * HEAD_DIM // LANES
    ok2, kcat, v_t = [], [], []
    for u in range(nq):
        sb = pl.program_id(1) * nq + u
        lo_prev = jnp.where(sb == 0, META_ROW0, jnp.where(sb == nblk - 1, BLK, 0))
        lo_cur = jnp.where(sb == nblk - 1, META_ROW0, 0)
        lo_next = jnp.where(sb == nblk - 2, BLK, 0)
        lo = jnp.where(seg == 0, lo_prev, jnp.where(seg == 1, lo_cur, lo_next))
        ok = band & (row >= lo)
        ok2.append(jnp.concatenate([ok, ok], axis=1))
        for m in range(n_pairs):
            cols = slice(m * LANES, (m + 1) * LANES)
            kcat.append(jnp.concatenate([r[0, :, cols] for r in k_refs[u:u + 3]], axis=0))
            vcat = jnp.concatenate([r[0, :, cols] for r in v_refs[u:u + 3]], axis=0)
            v_t.append(jnp.concatenate([jnp.transpose(vcat.astype(F32)).astype(BF16),
                                        jnp.ones((DEN_ROWS, 3 * BLK), BF16)], axis=0))

    per_blk = n_pairs * groups

    def scores(n):
        u, c = divmod(n, per_blk)
        qc = q_refs[u][0, :, c * LANES:(c + 1) * LANES]
        lhs = jnp.concatenate([jnp.where(first, qc, zero), jnp.where(first, zero, qc)], axis=0)
        return _dot_nt(kcat[u * n_pairs + c // groups], lhs)

    n_chunks = nq * per_blk
    pending = [scores(n) for n in range(ATTN_LOOKAHEAD)]
    for n in range(n_chunks):
        u, c = divmod(n, per_blk)
        m, i = divmod(c, groups)
        if n + ATTN_LOOKAHEAD < n_chunks:
            pending.append(scores(n + ATTN_LOOKAHEAD))
        s_t = jnp.where(ok2[u], pending.pop(0), NEG)
        sink = sink_ref[m, :, 2 * i * BLK:(2 * i + 2) * BLK]
        mx = jnp.maximum(jnp.max(s_t, axis=0, keepdims=True), sink)
        p = jnp.exp2(s_t - mx)
        o_t = _dot(v_t[u * n_pairs + m], p.astype(BF16))
        den = o_t[LANES:LANES + 1, :] + jnp.exp2(sink - mx)
        o_t = o_t[:LANES, :] / den
        blk = jnp.where(first_rows, o_t[:, :BLK], o_t[:, BLK:])
        o_ref[0, u * BLK:(u + 1) * BLK, c * LANES:(c + 1) * LANES] = jnp.transpose(blk).astype(BF16)


def _stage_b(q, k, v, sink_rows):
    b, li, qw = q.shape
    kvw = k.shape[-1]
    nblk = li // BLK
    nq = ATTN_BLOCKS
    q_specs = [pl.BlockSpec((1, BLK, qw), lambda bi, j, u=u: (bi, jnp.minimum(j * nq + u, nblk - 1), 0))
               for u in range(nq)]
    kv_specs = [pl.BlockSpec((1, BLK, kvw), lambda bi, j, u=u: (bi, (j * nq + u + nblk - 1) % nblk, 0))
                for u in range(nq + 2)]
    return pl.pallas_call(
        functools.partial(_attn_kernel, nblk=nblk),
        grid=(b, -(-nblk // nq)),
        in_specs=q_specs + kv_specs + kv_specs
        + [pl.BlockSpec(sink_rows.shape, lambda bi, j: (0, 0, 0))],
        out_specs=pl.BlockSpec((1, nq * BLK, qw), lambda bi, j: (bi, j, 0)),
        out_shape=jax.ShapeDtypeStruct((b, li, qw), BF16),
        compiler_params=_cparams(2),
        name="attn_stage",
    )(*([q] * nq + [k] * (nq + 2) + [v] * (nq + 2) + [sink_rows]))


def _mix_kernel(x_ref, meta_ref, attn_ref, cb_ref, s_ref, sp_ref, sn_ref, ga_ref, gc_ref,
                wconv_ref, gffn_ref, wa_ref, wc_ref, wo_ref, wr_ref,
                x1_ref, h2_ref, aff_ref, *, nreal):
    j = pl.program_id(1)

    def body(x, rows, n_pad):
        attn_p = _dot(attn_ref[0, :rows, :], wa_ref[...])
        s = s_ref[0, :rows, :].astype(F32)
        ridx = lax.broadcasted_iota(jnp.int32, (rows, 1), 0)
        s_prev = jnp.where(ridx == 0, sp_ref[0, HALO - 1:HALO, :].astype(F32), pltpu.roll(s, 1, 0))
        s_next = jnp.where(ridx == rows - 1, sn_ref[0, 0:1, :].astype(F32), pltpu.roll(s, rows - 1, 0))
        conv = wconv_ref[0:1, :] * s_prev + wconv_ref[1:2, :] * s + wconv_ref[2:3, :] * s_next
        conv_in = (cb_ref[0, :rows, :].astype(F32) * conv).astype(BF16)
        conv_p = _dot(conv_in, wc_ref[...])
        merged = ga_ref[0, :rows, :].astype(F32) * attn_p + gc_ref[0, :rows, :].astype(F32) * conv_p
        x1 = x + _dot(merged.astype(BF16), wo_ref[...])
        x1_ref[0, :rows, :] = x1
        ms = jnp.mean(x1 * x1, axis=-1, keepdims=True)
        h2 = (x1 * lax.rsqrt(ms + EPS) * gffn_ref[...]).astype(BF16)
        h2_ref[0, :rows, :] = h2
        logits = _dot_nt(wr_ref[...], h2)
        e = jnp.exp(logits - jnp.max(logits, axis=0, keepdims=True))
        aff = e / jnp.sum(e, axis=0, keepdims=True)
        if n_pad:
            aff = jnp.where(lax.broadcasted_iota(jnp.int32, aff.shape, 1) < n_pad, -1.0, aff)
        aff_ref[0, :, :rows] = aff

    @pl.when(j < nreal)
    def _():
        body(x_ref[0], TM, 0)

    @pl.when(j == nreal)
    def _():
        body(meta_ref[...], BLK, META_ROW0)


def _stage_c(x, meta_blk, attn, cb, sprod, ga, gc, wconv, gffn, wa, wc, wo, wr_t):
    b, s, d = x.shape
    li = s + BLK
    nreal = s // TM
    nhalo = li // HALO
    const = lambda bi, j: (0, 0)
    rowblk = lambda bi, j: (bi, j, 0)
    halo_prev = lambda bi, j: (bi, (j * (TM // HALO) + nhalo - 1) % nhalo, 0)
    halo_next = lambda bi, j: (bi, jnp.where(j == nreal, 0, (j + 1) * (TM // HALO)), 0)
    n_exp = wr_t.shape[0]
    wspec = lambda w: pl.BlockSpec(w.shape, const, pipeline_mode=pl.Buffered(1))
    return pl.pallas_call(
        functools.partial(_mix_kernel, nreal=nreal),
        grid=(b, nreal + 1),
        in_specs=[
            pl.BlockSpec((1, TM, d), lambda bi, j: (bi, jnp.minimum(j, nreal - 1), 0)),
            pl.BlockSpec((BLK, d), const),
            pl.BlockSpec((1, TM, d), rowblk),
            pl.BlockSpec((1, TM, d), rowblk),
            pl.BlockSpec((1, TM, d), rowblk),
            pl.BlockSpec((1, HALO, d), halo_prev),
            pl.BlockSpec((1, HALO, d), halo_next),
            pl.BlockSpec((1, TM, d), rowblk),
            pl.BlockSpec((1, TM, d), rowblk),
            pl.BlockSpec(wconv.shape, const),
            pl.BlockSpec((1, d), const),
            wspec(wa), wspec(wc), wspec(wo), wspec(wr_t),
        ],
        out_specs=[
            pl.BlockSpec((1, TM, d), rowblk),
            pl.BlockSpec((1, TM, d), rowblk),
            pl.BlockSpec((1, n_exp, TM), lambda bi, j: (bi, 0, j)),
        ],
        out_shape=[
            jax.ShapeDtypeStruct((b, li, d), F32),
            jax.ShapeDtypeStruct((b, li, d), BF16),
            jax.ShapeDtypeStruct((b, n_exp, li), F32),
        ],
        compiler_params=_cparams(2),
        name="mix_stage",
    )(x, meta_blk, attn, cb, sprod, sprod, sprod, ga, gc, wconv, gffn, wa, wc, wo, wr_t)


def _expert_kernel(xs_ref, wg_ref, wu_ref, wd_ref, y_ref):
    x = xs_ref[0]
    g = _dot(x, wg_ref[0])
    u = _dot(x, wu_ref[0])
    he = (g * jax.nn.sigmoid(g) * u).astype(BF16)
    y_ref[0] = _dot(he, wd_ref[0]).astype(y_ref.dtype)


def _stage_d(xs, wg, wu, wd):
    e, cap_p, d = xs.shape
    de = wg.shape[-1]
    tile = lambda ei, t: (ei, t, 0)
    wmap = lambda ei, t: (ei, 0, 0)
    return pl.pallas_call(
        _expert_kernel,
        grid=(e, cap_p // TE),
        in_specs=[
            pl.BlockSpec((1, TE, d), tile),
            pl.BlockSpec((1, d, de), wmap),
            pl.BlockSpec((1, d, de), wmap),
            pl.BlockSpec((1, de, d), wmap),
        ],
        out_specs=pl.BlockSpec((1, TE, d), tile),
        out_shape=jax.ShapeDtypeStruct((e, cap_p, d), BF16),
        compiler_params=_cparams(2),
        name="expert_stage",
    )(xs, wg, wu, wd)


def _combine_kernel(start_ref, rounds_ref, x1_ref, slot_ref, gate_ref, ye_ref, o_ref, ybuf, sem, *,
                    n_exp, nsteps):
    lin = pl.program_id(0) * pl.num_programs(1) + pl.program_id(1)

    def chunk_start(step, e, rnd):
        return jnp.minimum(start_ref[step * n_exp + e] + rnd * CH, ye_ref.shape[1] - CH)

    def chunk_copy(step, e, rnd, buf):
        st = pl.multiple_of(chunk_start(step, e, rnd), HALO)
        return pltpu.make_async_copy(ye_ref.at[e, pl.ds(st, CH), :],
                                     ybuf.at[buf, pl.ds(e * CH, CH), :], sem.at[buf])

    def fetch(step, rnd, buf):
        for e in range(n_exp):
            chunk_copy(step, e, rnd, buf).start()

    def wait(step, rnd, buf):
        for e in range(n_exp):
            chunk_copy(step, e, rnd, buf).wait()

    buf = lin % 2

    @pl.when(lin == 0)
    def _():
        fetch(0, 0, 0)

    @pl.when(lin + 1 < nsteps)
    def _():
        fetch(lin + 1, 0, 1 - buf)

    slot = slot_ref[0]
    gate = gate_ref[0]
    lane_c = lax.broadcasted_iota(jnp.int32, (1, CH), 1)

    def gathered(rnd):
        perm = []
        for e in range(n_exp):
            st = chunk_start(lin, e, rnd)
            first_lane = start_ref[lin * n_exp + e] + rnd * CH - st
            lanes = jnp.where(lane_c >= first_lane, lane_c, NO_LANE)
            hit = slot[:, e:e + 1] - st == lanes
            perm.append(jnp.where(hit, gate[:, e:e + 1], 0.0).astype(BF16))
        return _dot(jnp.concatenate(perm, axis=1), ybuf[buf])

    wait(lin, 0, buf)
    o_ref[0] = x1_ref[0] + gathered(0)

    def extra(rnd, carry):
        fetch(lin, rnd, buf)
        wait(lin, rnd, buf)
        o_ref[0] += gathered(rnd)
        return carry

    lax.fori_loop(1, rounds_ref[lin], extra, 0)


def _stage_e(x1, slot, gate, ye, start_al, rounds, s):
    b, li, d = x1.shape
    n_exp = slot.shape[-1]
    ntile = s // TM
    tile = lambda bi, j, *_: (bi, j, 0)
    return pl.pallas_call(
        functools.partial(_combine_kernel, n_exp=n_exp, nsteps=b * ntile),
        grid_spec=pltpu.PrefetchScalarGridSpec(
            num_scalar_prefetch=2,
            grid=(b, ntile),
            in_specs=[
                pl.BlockSpec((1, TM, d), tile),
                pl.BlockSpec((1, TM, n_exp), tile),
                pl.BlockSpec((1, TM, n_exp), tile),
                pl.BlockSpec(memory_space=pl.ANY),
            ],
            out_specs=pl.BlockSpec((1, TM, d), tile),
            scratch_shapes=[pltpu.VMEM((2, n_exp * CH, d), BF16), pltpu.SemaphoreType.DMA((2,))],
        ),
        out_shape=jax.ShapeDtypeStruct((b, s, d), F32),
        compiler_params=_cparams(2),
        name="combine_stage",
    )(start_al, rounds, x1, slot, gate, ye)


def _q_head_perm():
    groups = N_HEADS // N_KV_HEADS
    cols = []
    for m in range(N_KV_HEADS // 2):
        for i in range(groups):
            for h in (2 * groups * m + i, 2 * groups * m + groups + i):
                cols.extend(range(h * HEAD_DIM, (h + 1) * HEAD_DIM))
    return np.asarray(cols, np.int32)


def _rope_tables(s):
    pos = jnp.concatenate([N_META + jnp.arange(s, dtype=F32), jnp.zeros((META_ROW0,), F32),
                           jnp.arange(N_META, dtype=F32)])
    inv = ROPE_THETA ** (-jnp.arange(0, HEAD_DIM, 2, dtype=F32) / HEAD_DIM)
    ang = pos[:, None] * inv[None, :]
    reps = LANES // (HEAD_DIM // 2)
    cos2 = jnp.tile(jnp.cos(ang), (1, reps))
    sgn = jnp.where((jnp.arange(LANES) % HEAD_DIM) < HEAD_DIM // 2, -1.0, 1.0).astype(F32)
    sin2 = jnp.tile(jnp.sin(ang), (1, reps)) * sgn[None, :]
    return cos2, sin2


def _sink_rows(sink):
    groups = N_HEADS // N_KV_HEADS
    heads = np.asarray([[2 * groups * m + groups * half + i
                         for i in range(groups) for half in range(2)]
                        for m in range(N_KV_HEADS // 2)], np.int32)
    rows = jnp.repeat(sink.astype(F32)[heads] * LOG2E, BLK, axis=1)
    return rows[:, None, :]


def _threshold_kernel(aff_ref, thr_ref, *, cap):
    bits = pltpu.bitcast(aff_ref[...], jnp.int32)

    def step(i, prefix):
        cand = prefix | jnp.left_shift(jnp.int32(1), 30 - i)
        count = jnp.sum((bits >= cand).astype(jnp.int32), axis=1, keepdims=True)
        return jnp.where(count >= cap, cand, prefix)

    prefix = lax.fori_loop(0, 31, step, jnp.zeros((bits.shape[0], 1), jnp.int32))
    thr_ref[...] = jnp.broadcast_to(pltpu.bitcast(prefix, F32), thr_ref.shape)


def _thresholds(aff, cap):
    n_exp = aff.shape[0]
    out = pl.pallas_call(
        functools.partial(_threshold_kernel, cap=cap),
        out_shape=jax.ShapeDtypeStruct((n_exp, LANES), F32),
        compiler_params=pltpu.CompilerParams(vmem_limit_bytes=VMEM_LIMIT),
        name="threshold_stage",
    )(aff)
    return out[:, 0]


def _prefix_counts(mask):
    e, n = mask.shape
    g = n // LANES
    tri_l = jnp.asarray(np.tri(LANES, dtype=np.float32).T, BF16)
    tri_g = jnp.asarray(np.tri(g, k=-1, dtype=np.float32).T, BF16)
    local = jnp.einsum("egl,lm->egm", mask.reshape(e, g, LANES).astype(BF16), tri_l,
                       preferred_element_type=F32)
    before = jnp.einsum("eg,gh->eh", local[:, :, LANES - 1].astype(BF16), tri_g,
                        preferred_element_type=F32)
    return local.astype(jnp.int32), before.astype(jnp.int32)


def _route(aff_t, s):
    b, n_exp, li = aff_t.shape
    n = b * li
    ngrp = n // LANES
    n_tok = b * (s + N_META)
    cap = max(1, CAPACITY_FACTOR * n_tok // n_exp)
    cap_p = max(-(-cap // TE) * TE, CH)
    aff = jnp.transpose(aff_t, (1, 0, 2)).reshape(n_exp, n)
    thr = _thresholds(aff, cap)[:, None]
    above = aff > thr
    equal = aff == thr
    room = cap - jnp.sum(above, axis=1, dtype=jnp.int32, keepdims=True)

    def earliest_ties(_):
        local, before = _prefix_counts(equal)
        eq_upto = (local + before[:, :, None]).reshape(n_exp, b, li)
        eq_real_end = eq_upto[:, :, s - 1:s]
        eq_before_seq = eq_upto[:, :, 0:1] - equal.reshape(n_exp, b, li)[:, :, 0:1]
        eq_real = eq_real_end - eq_before_seq
        eq_meta = eq_upto[:, :, li - 1:li] - eq_real_end
        is_real = (jnp.arange(li) < s)[None, None, :]
        rank = (eq_upto + jnp.where(is_real, eq_meta, -eq_real)).reshape(n_exp, n)
        return above | (equal & (rank <= room))

    all_ties_fit = jnp.all(jnp.sum(equal, axis=1, dtype=jnp.int32, keepdims=True) == room)
    sel = lax.cond(all_ties_fit, lambda _: above | equal, earliest_ties, None)

    local, grp_before = _prefix_counts(sel)
    upto = (local + grp_before[:, :, None]).reshape(n_exp, n)
    slot = jnp.transpose(jnp.where(sel, upto - 1, UNSELECTED)).reshape(b, li, n_exp)
    gate = jnp.transpose(jnp.where(sel, aff, 0.0)).reshape(b, li, n_exp)

    tile_grp = TM // LANES
    seq_before = grp_before.reshape(n_exp, b, li // LANES)
    start = seq_before[:, :, 0:s // LANES:tile_grp]
    end = seq_before[:, :, tile_grp:s // LANES + 1:tile_grp]
    start_al = start // HALO * HALO
    rounds = jnp.maximum(jnp.max(-(-(end - start_al) // CH), axis=0), 1)
    start_al = jnp.transpose(start_al, (1, 2, 0)).reshape(-1)

    grp_end = jnp.concatenate([grp_before[:, 1:], jnp.full((n_exp, 1), cap, jnp.int32)], axis=1)
    gidx = jnp.arange(ngrp, dtype=jnp.int32)[None, :]
    byte = lambda v: jnp.stack([v // 256, v % 256], axis=-1)
    table = jnp.concatenate([local, byte(grp_before), jnp.broadcast_to(byte(gidx), (n_exp, ngrp, 2))],
                            axis=-1).astype(BF16)
    slots = jnp.arange(cap_p, dtype=jnp.int32)[None, :, None]
    in_grp = ((grp_before[:, None, :] <= slots) & (slots < grp_end[:, None, :])).astype(BF16)
    hit = jnp.einsum("esg,egc->esc", in_grp, table, preferred_element_type=F32).astype(jnp.int32)
    j = slots[:, :, 0] - (hit[:, :, LANES] * 256 + hit[:, :, LANES + 1])
    lane = jnp.sum(hit[:, :, :LANES] <= j[:, :, None], axis=-1, dtype=jnp.int32)
    rows = (hit[:, :, LANES + 2] * 256 + hit[:, :, LANES + 3]) * LANES + lane
    rows = jnp.where(slots[:, :, 0] < cap, rows, 0)
    return rows, slot, gate, start_al, rounds.reshape(-1)


def _project(x, meta_blk, prm, cast_srcs=()):
    b, s, d = x.shape
    assert s % TM == 0 and d % LANES == 0
    cos2, sin2 = _rope_tables(s)
    return _stage_a(x, meta_blk, cos2, sin2, prm["gmix"], prm["gq2"], prm["gk2"], prm["bd"],
                    prm["w_in"], cast_srcs)


def _encoder(x, meta_blk, prm, projected, expert_w):
    b, s, d = x.shape
    li = s + BLK
    q, k, v, cb, sprod, ga, gc = projected
    attn = _stage_b(q, k, v, prm["sink_rows"])
    x1, h2, aff_t = _stage_c(x, meta_blk, attn, cb, sprod, ga, gc, prm["wconv"], prm["gffn"],
                             prm["wa"], prm["wc"], prm["wo"], prm["wr_t"])
    rows, slot, gate, start_al, rounds = _route(aff_t, s)
    xs = h2.reshape(b * li, d)[rows]
    ye = _stage_d(xs, *expert_w)
    return _stage_e(x1, slot, gate, ye, start_al, rounds, s)


def kernel(x_prompt, x_sample, meta_tokens, g_mix, w_in, g_q, g_k, sink_logits, w_conv, w_attn_out,
           w_conv_out, w_out, g_ffn, w_router, w_expert_gate, w_expert_up, w_expert_down):
    assert w_in.shape[0] == 1, "single-layer block"
    d = x_prompt.shape[-1]
    qw = N_HEADS * HEAD_DIM
    perm = _q_head_perm()
    w_in0 = w_in[0]
    lane = np.arange(LANES)
    prm = {
        "w_in": jnp.concatenate([w_in0[:, :qw][:, perm], w_in0[:, qw:]], axis=1).astype(BF16),
        "gmix": g_mix[0][None, :].astype(F32),
        "gq2": jnp.tile(g_q[0], LANES // HEAD_DIM)[None, :].astype(F32),
        "gk2": jnp.tile(g_k[0], LANES // HEAD_DIM)[None, :].astype(F32),
        "bd": jnp.asarray((lane[:, None] // HEAD_DIM) == (lane[None, :] // HEAD_DIM), BF16),
        "sink_rows": _sink_rows(sink_logits[0]),
        "wconv": w_conv[0].astype(F32),
        "gffn": g_ffn[0][None, :].astype(F32),
        "wa": w_attn_out[0][perm, :].astype(BF16),
        "wc": w_conv_out[0].astype(BF16),
        "wo": w_out[0].astype(BF16),
        "wr_t": jnp.transpose(w_router[0]).astype(BF16),
    }
    meta_blk = jnp.concatenate([jnp.zeros((META_ROW0, d), F32), meta_tokens.astype(F32)], axis=0)
    expert_w = [w_expert_gate[0], w_expert_up[0], w_expert_down[0]]
    flat = [w.astype(F32).reshape(-1, w.shape[-1]) for w in expert_w]
    steps = x_prompt.shape[0] * (x_prompt.shape[1] // TM)
    if all(_cast_rows(w, steps) for w in flat):
        proj_prompt, cast = _project(x_prompt, meta_blk, prm, flat)
        expert_w = [c.reshape(w.shape) for c, w in zip(cast, expert_w)]
    else:
        proj_prompt, _ = _project(x_prompt, meta_blk, prm)
        expert_w = [w.astype(BF16) for w in expert_w]
    proj_sample, _ = _project(x_sample, meta_blk, prm)
    return (_encoder(x_prompt, meta_blk, prm, proj_prompt, expert_w),
            _encoder(x_sample, meta_blk, prm, proj_sample, expert_w))
```

```python
import functools

import numpy as np
import jax
import jax.numpy as jnp
from jax import lax
from jax.experimental import pallas as pl
from jax.experimental.pallas import tpu as pltpu

N_META = 16
N_HEADS = 16
N_KV_HEADS = 4
HEAD_DIM = 64
CAPACITY_FACTOR = 2
ROPE_THETA = 10000.0
EPS = 1e-6
NEG = -1e30

LANES = 128
BLK = 128
META_ROW0 = BLK - N_META
TM = 512
TE = 1056
HALO = 16
DEN_ROWS = 16
ATTN_LOOKAHEAD = 4
ATTN_BLOCKS = 3
CH = 128
UNSELECTED = -(1 << 30)
NO_LANE = 1 << 30
BYTE = 256
LOG2E = 1.4426950408889634
Q_SCALE = HEAD_DIM ** -0.5 * LOG2E
VMEM_LIMIT = 56 * 1024 * 1024
CAST_CHUNK_BYTES = 2 * 1024 * 1024

F32 = jnp.float32
BF16 = jnp.bfloat16


def _dot(a, b):
    return jnp.dot(a, b, preferred_element_type=F32)


def _dot_nt(a, b):
    return lax.dot_general(a, b, (((1,), (1,)), ((), ())), preferred_element_type=F32)


def _cparams(n_axes):
    return pltpu.CompilerParams(dimension_semantics=("arbitrary",) * n_axes,
                                vmem_limit_bytes=VMEM_LIMIT)


def _proj_kernel(x_ref, meta_ref, cos_ref, sin_ref, gmix_ref, gq_ref, gk_ref, bd_ref, w_ref, *refs,
                 nreal, d, qw, kvw, cw, n_cast):
    cast_in, refs = refs[:n_cast], refs[n_cast:]
    (q_ref, k_ref, v_ref, cb_ref, s_ref, ga_ref, gc_ref), cast_out = refs[:7], refs[7:]
    j = pl.program_id(1)
    lane = lax.broadcasted_iota(jnp.int32, (1, LANES), 1)
    first_half = jnp.bitwise_and(lane, HEAD_DIM - 1) < (HEAD_DIM // 2)

    def norm_rope(t, ss, g, cos, sin):
        t = t * lax.rsqrt(ss * (1.0 / HEAD_DIM) + EPS) * g
        partner = jnp.where(first_half, pltpu.roll(t, LANES - HEAD_DIM // 2, 1),
                            pltpu.roll(t, HEAD_DIM // 2, 1))
        return t * cos + partner * sin

    def body(x, cos, sin, rows):
        ms = jnp.mean(x * x, axis=-1, keepdims=True)
        h = (x * lax.rsqrt(ms + EPS) * gmix_ref[...]).astype(BF16)
        qk = _dot(h, w_ref[:, :qw + kvw])
        off = qw + kvw
        v_ref[0, :rows, :] = _dot(h, w_ref[:, off:off + kvw]).astype(BF16)
        off += kvw
        chunks = [qk[:, c * LANES:(c + 1) * LANES] for c in range((qw + kvw) // LANES)]
        sumsq = [_dot((t * t).astype(BF16), bd_ref[...]) for t in chunks]
        cb_ref[0, :rows, :] = _dot(h, w_ref[:, off:off + cw]).astype(BF16)
        off += cw
        cc = _dot(h, w_ref[:, off:off + cw])
        off += cw
        cu = _dot(h, w_ref[:, off:off + cw])
        off += cw
        s_ref[0, :rows, :] = (cc * cu).astype(BF16)
        ga_ref[0, :rows, :] = jax.nn.sigmoid(_dot(h, w_ref[:, off:off + d])).astype(BF16)
        off += d
        gc_ref[0, :rows, :] = jax.nn.sigmoid(_dot(h, w_ref[:, off:off + d])).astype(BF16)
        for c, (t, ss) in enumerate(zip(chunks, sumsq)):
            is_q = c < qw // LANES
            t = norm_rope(t, ss, (gq_ref if is_q else gk_ref)[...], cos, sin)
            if is_q:
                q_ref[0, :rows, c * LANES:(c + 1) * LANES] = (t * Q_SCALE).astype(BF16)
            else:
                ck = c - qw // LANES
                k_ref[0, :rows, ck * LANES:(ck + 1) * LANES] = t.astype(BF16)

    @pl.when(j < nreal)
    def _():
        body(x_ref[0], cos_ref[...], sin_ref[...], TM)
        for src, dst in zip(cast_in, cast_out):
            dst[...] = src[...].astype(BF16)

    @pl.when(j == nreal)
    def _():
        body(meta_ref[...], cos_ref[:BLK, :], sin_ref[:BLK, :], BLK)


def _cast_rows(arr, steps):
    rows, cols = arr.shape
    r = rows // steps
    ok = rows % steps == 0 and r % HALO == 0 and r * cols * 4 <= CAST_CHUNK_BYTES
    return r if ok else 0


def _stage_a(x, meta_blk, cos2, sin2, gmix, gq2, gk2, bd, w_in_b, cast_srcs=()):
    b, s, d = x.shape
    li = s + BLK
    nreal = s // TM
    qw, kvw, cw = N_HEADS * HEAD_DIM, N_KV_HEADS * HEAD_DIM, d
    const = lambda bi, j: (0, 0)
    rowblk = lambda bi, j: (bi, j, 0)
    out_w = (qw, kvw, kvw, cw, cw, d, d)
    chunk = lambda bi, j: (bi * nreal + jnp.minimum(j, nreal - 1), 0)
    cast_specs = [pl.BlockSpec((_cast_rows(a, b * nreal), a.shape[1]), chunk) for a in cast_srcs]
    outs = pl.pallas_call(
        functools.partial(_proj_kernel, nreal=nreal, d=d, qw=qw, kvw=kvw, cw=cw, n_cast=len(cast_srcs)),
        grid=(b, nreal + 1),
        in_specs=[
            pl.BlockSpec((1, TM, d), lambda bi, j: (bi, jnp.minimum(j, nreal - 1), 0)),
            pl.BlockSpec((BLK, d), const),
            pl.BlockSpec((TM, LANES), lambda bi, j: (j, 0)),
            pl.BlockSpec((TM, LANES), lambda bi, j: (j, 0)),
            pl.BlockSpec((1, d), const),
            pl.BlockSpec((1, LANES), const),
            pl.BlockSpec((1, LANES), const),
            pl.BlockSpec((LANES, LANES), const),
            pl.BlockSpec(w_in_b.shape, const, pipeline_mode=pl.Buffered(1)),
        ] + cast_specs,
        out_specs=[pl.BlockSpec((1, TM, w), rowblk) for w in out_w] + cast_specs,
        out_shape=[jax.ShapeDtypeStruct((b, li, w), BF16) for w in out_w]
        + [jax.ShapeDtypeStruct(a.shape, BF16) for a in cast_srcs],
        compiler_params=_cparams(2),
        name="proj_stage",
    )(x, meta_blk, cos2, sin2, gmix, gq2, gk2, bd, w_in_b, *cast_srcs)
    return outs[:7], outs[7:]


def _attn_kernel(*refs, nblk):
    nq, nkv = ATTN_BLOCKS, ATTN_BLOCKS + 2
    q_refs, k_refs, v_refs = refs[:nq], refs[nq:nq + nkv], refs[nq + nkv:nq + 2 * nkv]
    sink_ref, o_ref = refs[nq + 2 * nkv:]
    kj = lax.broadcasted_iota(jnp.int32, (3 * BLK, BLK), 0)
    qi = lax.broadcasted_iota(jnp.int32, (3 * BLK, BLK), 1)
    band = (kj >= qi) & (kj <= qi + 2 * BLK)
    seg = jnp.right_shift(kj, BLK.bit_length() - 1)
    row = jnp.bitwise_and(kj, BLK - 1)
    lane = lax.broadcasted_iota(jnp.int32, (1, LANES), 1)
    first = lane < HEAD_DIM
    first_rows = lax.broadcasted_iota(jnp.int32, (LANES, 1), 0) < HEAD_DIM
    zero = jnp.zeros((), BF16)
    groups = N_HEADS // N_KV_HEADS
    n_pairs = N_KV_HEADS * HEAD_DIM // LANES
    ok2, kcat, v_t = [], [], []
    for u in range(nq):
        sb = pl.program_id(1) * nq + u
        lo_prev = jnp.where(sb == 0, META_ROW0, jnp.where(sb == nblk - 1, BLK, 0))
        lo_cur = jnp.where(sb == nblk - 1, META_ROW0, 0)
        lo_next = jnp.where(sb == nblk - 2, BLK, 0)
        lo = jnp.where(seg == 0, lo_prev, jnp.where(seg == 1, lo_cur, lo_next))
        ok = band & (row >= lo)
        ok2.append(jnp.concatenate([ok, ok], axis=1))
        for m in range(n_pairs):
            cols = slice(m * LANES, (m + 1) * LANES)
            kcat.append(jnp.concatenate([r[0, :, cols] for r in k_refs[u:u + 3]], axis=0))
            vcat = jnp.concatenate([r[0, :, cols] for r in v_refs[u:u + 3]], axis=0)
            v_t.append(jnp.concatenate([jnp.transpose(vcat.astype(F32)).astype(BF16),
                                        jnp.ones((DEN_ROWS, 3 * BLK), BF16)], axis=0))

    per_blk = n_pairs * groups

    def scores(n):
        u, c = divmod(n, per_blk)
        qc = q_refs[u][0, :, c * LANES:(c + 1) * LANES]
        lhs = jnp.concatenate([jnp.where(first, qc, zero), jnp.where(first, zero, qc)], axis=0)
        return _dot_nt(kcat[u * n_pairs + c // groups], lhs)

    n_chunks = nq * per_blk
    pending = [scores(n) for n in range(ATTN_LOOKAHEAD)]
    for n in range(n_chunks):
        u, c = divmod(n, per_blk)
        m, i = divmod(c, groups)
        if n + ATTN_LOOKAHEAD < n_chunks:
            pending.append(scores(n + ATTN_LOOKAHEAD))
        s_t = jnp.where(ok2[u], pending.pop(0), NEG)
        sink = sink_ref[m, :, 2 * i * BLK:(2 * i + 2) * BLK]
        mx = jnp.maximum(jnp.max(s_t, axis=0, keepdims=True), sink)
        p = jnp.exp2(s_t - mx)
        o_t = _dot(v_t[u * n_pairs + m], p.astype(BF16))
        den = o_t[LANES:LANES + 1, :] + jnp.exp2(sink - mx)
        o_t = o_t[:LANES, :] / den
        blk = jnp.where(first_rows, o_t[:, :BLK], o_t[:, BLK:])
        o_ref[0, u * BLK:(u + 1) * BLK, c * LANES:(c + 1) * LANES] = jnp.transpose(blk).astype(BF16)


def _stage_b(q, k, v, sink_rows):
    b, li, qw = q.shape
    kvw = k.shape[-1]
    nblk = li // BLK
    nq = ATTN_BLOCKS
    q_specs = [pl.BlockSpec((1, BLK, qw), lambda bi, j, u=u: (bi, jnp.minimum(j * nq + u, nblk - 1), 0))
               for u in range(nq)]
    kv_specs = [pl.BlockSpec((1, BLK, kvw), lambda bi, j, u=u: (bi, (j * nq + u + nblk - 1) % nblk, 0))
                for u in range(nq + 2)]
    return pl.pallas_call(
        functools.partial(_attn_kernel, nblk=nblk),
        grid=(b, -(-nblk // nq)),
        in_specs=q_specs + kv_specs + kv_specs
        + [pl.BlockSpec(sink_rows.shape, lambda bi, j: (0, 0, 0))],
        out_specs=pl.BlockSpec((1, nq * BLK, qw), lambda bi, j: (bi, j, 0)),
        out_shape=jax.ShapeDtypeStruct((b, li, qw), BF16),
        compiler_params=_cparams(2),
        name="attn_stage",
    )(*([q] * nq + [k] * (nq + 2) + [v] * (nq + 2) + [sink_rows]))


def _mix_kernel(x_ref, meta_ref, attn_ref, cb_ref, s_ref, sp_ref, sn_ref, ga_ref, gc_ref,
                wconv_ref, gffn_ref, wa_ref, wc_ref, wo_ref, wr_ref,
                x1_ref, h2_ref, aff_ref, *, nreal):
    j = pl.program_id(1)

    def body(x, rows, n_pad):
        attn_p = _dot(attn_ref[0, :rows, :], wa_ref[...])
        s = s_ref[0, :rows, :].astype(F32)
        ridx = lax.broadcasted_iota(jnp.int32, (rows, 1), 0)
        s_prev = jnp.where(ridx == 0, sp_ref[0, HALO - 1:HALO, :].astype(F32), pltpu.roll(s, 1, 0))
        s_next = jnp.where(ridx == rows - 1, sn_ref[0, 0:1, :].astype(F32), pltpu.roll(s, rows - 1, 0))
        conv = wconv_ref[0:1, :] * s_prev + wconv_ref[1:2, :] * s + wconv_ref[2:3, :] * s_next
        conv_in = (cb_ref[0, :rows, :].astype(F32) * conv).astype(BF16)
        conv_p = _dot(conv_in, wc_ref[...])
        merged = ga_ref[0, :rows, :].astype(F32) * attn_p + gc_ref[0, :rows, :].astype(F32) * conv_p
        x1 = x + _dot(merged.astype(BF16), wo_ref[...])
        x1_ref[0, :rows, :] = x1
        ms = jnp.mean(x1 * x1, axis=-1, keepdims=True)
        h2 = (x1 * lax.rsqrt(ms + EPS) * gffn_ref[...]).astype(BF16)
        h2_ref[0, :rows, :] = h2
        logits = _dot_nt(wr_ref[...], h2)
        e = jnp.exp(logits - jnp.max(logits, axis=0, keepdims=True))
        aff = e / jnp.sum(e, axis=0, keepdims=True)
        if n_pad:
            aff = jnp.where(lax.broadcasted_iota(jnp.int32, aff.shape, 1) < n_pad, -1.0, aff)
        aff_ref[0, :, :rows] = aff

    @pl.when(j < nreal)
    def _():
        body(x_ref[0], TM, 0)

    @pl.when(j == nreal)
    def _():
        body(meta_ref[...], BLK, META_ROW0)


def _stage_c(x, meta_blk, attn, cb, sprod, ga, gc, wconv, gffn, wa, wc, wo, wr_t):
    b, s, d = x.shape
    li = s + BLK
    nreal = s // TM
    nhalo = li // HALO
    const = lambda bi, j: (0, 0)
    rowblk = lambda bi, j: (bi, j, 0)
    halo_prev = lambda bi, j: (bi, (j * (TM // HALO) + nhalo - 1) % nhalo, 0)
    halo_next = lambda bi, j: (bi, jnp.where(j == nreal, 0, (j + 1) * (TM // HALO)), 0)
    n_exp = wr_t.shape[0]
    wspec = lambda w: pl.BlockSpec(w.shape, const, pipeline_mode=pl.Buffered(1))
    return pl.pallas_call(
        functools.partial(_mix_kernel, nreal=nreal),
        grid=(b, nreal + 1),
        in_specs=[
            pl.BlockSpec((1, TM, d), lambda bi, j: (bi, jnp.minimum(j, nreal - 1), 0)),
            pl.BlockSpec((BLK, d), const),
            pl.BlockSpec((1, TM, d), rowblk),
            pl.BlockSpec((1, TM, d), rowblk),
            pl.BlockSpec((1, TM, d), rowblk),
            pl.BlockSpec((1, HALO, d), halo_prev),
            pl.BlockSpec((1, HALO, d), halo_next),
            pl.BlockSpec((1, TM, d), rowblk),
            pl.BlockSpec((1, TM, d), rowblk),
            pl.BlockSpec(wconv.shape, const),
            pl.BlockSpec((1, d), const),
            wspec(wa), wspec(wc), wspec(wo), wspec(wr_t),
        ],
        out_specs=[
            pl.BlockSpec((1, TM, d), rowblk),
            pl.BlockSpec((1, TM, d), rowblk),
            pl.BlockSpec((1, n_exp, TM), lambda bi, j: (bi, 0, j)),
        ],
        out_shape=[
            jax.ShapeDtypeStruct((b, li, d), F32),
            jax.ShapeDtypeStruct((b, li, d), BF16),
            jax.ShapeDtypeStruct((b, n_exp, li), F32),
        ],
        compiler_params=_cparams(2),
        name="mix_stage",
    )(x, meta_blk, attn, cb, sprod, sprod, sprod, ga, gc, wconv, gffn, wa, wc, wo, wr_t)


def _expert_kernel(xs_ref, wg_ref, wu_ref, wd_ref, y_ref):
    x = xs_ref[0]
    g = _dot(x, wg_ref[0])
    u = _dot(x, wu_ref[0])
    he = (g * jax.nn.sigmoid(g) * u).astype(BF16)
    y_ref[0] = _dot(he, wd_ref[0]).astype(y_ref.dtype)


def _stage_d(xs, wg, wu, wd):
    e, cap_p, d = xs.shape
    de = wg.shape[-1]
    tile = lambda ei, t: (ei, t, 0)
    wmap = lambda ei, t: (ei, 0, 0)
    return pl.pallas_call(
        _expert_kernel,
        grid=(e, cap_p // TE),
        in_specs=[
            pl.BlockSpec((1, TE, d), tile),
            pl.BlockSpec((1, d, de), wmap),
            pl.BlockSpec((1, d, de), wmap),
            pl.BlockSpec((1, de, d), wmap),
        ],
        out_specs=pl.BlockSpec((1, TE, d), tile),
        out_shape=jax.ShapeDtypeStruct((e, cap_p, d), BF16),
        compiler_params=_cparams(2),
        name="expert_stage",
    )(xs, wg, wu, wd)


def _combine_kernel(start_ref, rounds_ref, x1_ref, slot_ref, gate_ref, ye_ref, o_ref, ybuf, sem, *,
                    n_exp, nsteps):
    lin = pl.program_id(0) * pl.num_programs(1) + pl.program_id(1)

    def chunk_start(step, e, rnd):
        return jnp.minimum(start_ref[step * n_exp + e] + rnd * CH, ye_ref.shape[1] - CH)

    def chunk_copy(step, e, rnd, buf):
        st = pl.multiple_of(chunk_start(step, e, rnd), HALO)
        return pltpu.make_async_copy(ye_ref.at[e, pl.ds(st, CH), :],
                                     ybuf.at[buf, pl.ds(e * CH, CH), :], sem.at[buf])

    def fetch(step, rnd, buf):
        for e in range(n_exp):
            chunk_copy(step, e, rnd, buf).start()

    def wait(step, rnd, buf):
        for e in range(n_exp):
            chunk_copy(step, e, rnd, buf).wait()

    buf = lin % 2

    @pl.when(lin == 0)
    def _():
        fetch(0, 0, 0)

    @pl.when(lin + 1 < nsteps)
    def _():
        fetch(lin + 1, 0, 1 - buf)

    slot = slot_ref[0]
    gate = gate_ref[0]
    lane_c = lax.broadcasted_iota(jnp.int32, (1, CH), 1)

    def gathered(rnd):
        perm = []
        for e in range(n_exp):
            st = chunk_start(lin, e, rnd)
            first_lane = start_ref[lin * n_exp + e] + rnd * CH - st
            lanes = jnp.where(lane_c >= first_lane, lane_c, NO_LANE)
            hit = slot[:, e:e + 1] - st == lanes
            perm.append(jnp.where(hit, gate[:, e:e + 1], 0.0).astype(BF16))
        return _dot(jnp.concatenate(perm, axis=1), ybuf[buf])

    wait(lin, 0, buf)
    o_ref[0] = x1_ref[0] + gathered(0)

    def extra(rnd, carry):
        fetch(lin, rnd, buf)
        wait(lin, rnd, buf)
        o_ref[0] += gathered(rnd)
        return carry

    lax.fori_loop(1, rounds_ref[lin], extra, 0)


def _stage_e(x1, slot, gate, ye, start_al, rounds, s):
    b, li, d = x1.shape
    n_exp = slot.shape[-1]
    ntile = s // TM
    tile = lambda bi, j, *_: (bi, j, 0)
    return pl.pallas_call(
        functools.partial(_combine_kernel, n_exp=n_exp, nsteps=b * ntile),
        grid_spec=pltpu.PrefetchScalarGridSpec(
            num_scalar_prefetch=2,
            grid=(b, ntile),
            in_specs=[
                pl.BlockSpec((1, TM, d), tile),
                pl.BlockSpec((1, TM, n_exp), tile),
                pl.BlockSpec((1, TM, n_exp), tile),
                pl.BlockSpec(memory_space=pl.ANY),
            ],
            out_specs=pl.BlockSpec((1, TM, d), tile),
            scratch_shapes=[pltpu.VMEM((2, n_exp * CH, d), BF16), pltpu.SemaphoreType.DMA((2,))],
        ),
        out_shape=jax.ShapeDtypeStruct((b, s, d), F32),
        compiler_params=_cparams(2),
        name="combine_stage",
    )(start_al, rounds, x1, slot, gate, ye)


def _q_head_perm():
    groups = N_HEADS // N_KV_HEADS
    cols = []
    for m in range(N_KV_HEADS // 2):
        for i in range(groups):
            for h in (2 * groups * m + i, 2 * groups * m + groups + i):
                cols.extend(range(h * HEAD_DIM, (h + 1) * HEAD_DIM))
    return np.asarray(cols, np.int32)


def _rope_tables(s):
    pos = jnp.concatenate([N_META + jnp.arange(s, dtype=F32), jnp.zeros((META_ROW0,), F32),
                           jnp.arange(N_META, dtype=F32)])
    inv = ROPE_THETA ** (-jnp.arange(0, HEAD_DIM, 2, dtype=F32) / HEAD_DIM)
    ang = pos[:, None] * inv[None, :]
    reps = LANES // (HEAD_DIM // 2)
    cos2 = jnp.tile(jnp.cos(ang), (1, reps))
    sgn = jnp.where((jnp.arange(LANES) % HEAD_DIM) < HEAD_DIM // 2, -1.0, 1.0).astype(F32)
    sin2 = jnp.tile(jnp.sin(ang), (1, reps)) * sgn[None, :]
    return cos2, sin2


def _sink_rows(sink):
    groups = N_HEADS // N_KV_HEADS
    heads = np.asarray([[2 * groups * m + groups * half + i
                         for i in range(groups) for half in range(2)]
                        for m in range(N_KV_HEADS // 2)], np.int32)
    rows = jnp.repeat(sink.astype(F32)[heads] * LOG2E, BLK, axis=1)
    return rows[:, None, :]


def _threshold_kernel(aff_ref, thr_ref, *, cap):
    bits = pltpu.bitcast(aff_ref[...], jnp.int32)

    def step(i, prefix):
        cand = prefix | jnp.left_shift(jnp.int32(1), 30 - i)
        count = jnp.sum((bits >= cand).astype(jnp.int32), axis=1, keepdims=True)
        return jnp.where(count >= cap, cand, prefix)

    prefix = lax.fori_loop(0, 31, step, jnp.zeros((bits.shape[0], 1), jnp.int32))
    thr_ref[...] = jnp.broadcast_to(pltpu.bitcast(prefix, F32), thr_ref.shape)


def _thresholds(aff, cap):
    n_exp = aff.shape[0]
    out = pl.pallas_call(
        functools.partial(_threshold_kernel, cap=cap),
        out_shape=jax.ShapeDtypeStruct((n_exp, LANES), F32),
        compiler_params=pltpu.CompilerParams(vmem_limit_bytes=VMEM_LIMIT),
        name="threshold_stage",
    )(aff)
    return out[:, 0]


def _prefix_counts(mask):
    e, n = mask.shape
    g = n // LANES
    tri_l = jnp.asarray(np.tri(LANES, dtype=np.float32).T, BF16)
    tri_g = jnp.asarray(np.tri(g, k=-1, dtype=np.float32).T, BF16)
    local = jnp.einsum("egl,lm->egm", mask.reshape(e, g, LANES).astype(BF16), tri_l,
                       preferred_element_type=F32)
    before = jnp.einsum("eg,gh->eh", local[:, :, LANES - 1].astype(BF16), tri_g,
                        preferred_element_type=F32)
    return local.astype(jnp.int32), before.astype(jnp.int32)


def _route(aff_t, s):
    b, n_exp, li = aff_t.shape
    n = b * li
    ngrp = n // LANES
    n_tok = b * (s + N_META)
    cap = max(1, CAPACITY_FACTOR * n_tok // n_exp)
    cap_p = max(-(-cap // TE) * TE, CH)
    aff = jnp.transpose(aff_t, (1, 0, 2)).reshape(n_exp, n)
    thr = _thresholds(aff, cap)[:, None]
    above = aff > thr
    equal = aff == thr
    room = cap - jnp.sum(above, axis=1, dtype=jnp.int32, keepdims=True)

    def earliest_ties(_):
        local, before = _prefix_counts(equal)
        eq_upto = (local + before[:, :, None]).reshape(n_exp, b, li)
        eq_real_end = eq_upto[:, :, s - 1:s]
        eq_before_seq = eq_upto[:, :, 0:1] - equal.reshape(n_exp, b, li)[:, :, 0:1]
        eq_real = eq_real_end - eq_before_seq
        eq_meta = eq_upto[:, :, li - 1:li] - eq_real_end
        is_real = (jnp.arange(li) < s)[None, None, :]
        rank = (eq_upto + jnp.where(is_real, eq_meta, -eq_real)).reshape(n_exp, n)
        return above | (equal & (rank <= room))

    all_ties_fit = jnp.all(jnp.sum(equal, axis=1, dtype=jnp.int32, keepdims=True) == room)
    sel = lax.cond(all_ties_fit, lambda _: above | equal, earliest_ties, None)

    local, grp_before = _prefix_counts(sel)
    upto = (local + grp_before[:, :, None]).reshape(n_exp, n)
    slot = jnp.transpose(jnp.where(sel, upto - 1, UNSELECTED)).reshape(b, li, n_exp)
    gate = jnp.transpose(jnp.where(sel, aff, 0.0)).reshape(b, li, n_exp)

    tile_grp = TM // LANES
    seq_before = grp_before.reshape(n_exp, b, li // LANES)
    start = seq_before[:, :, 0:s // LANES:tile_grp]
    end = seq_before[:, :, tile_grp:s // LANES + 1:tile_grp]
    start_al = start // HALO * HALO
    rounds = jnp.maximum(jnp.max(-(-(end - start_al) // CH), axis=0), 1)
    start_al = jnp.transpose(start_al, (1, 2, 0)).reshape(-1)

    grp_end = jnp.concatenate([grp_before[:, 1:], jnp.full((n_exp, 1), cap, jnp.int32)], axis=1)
    gidx = jnp.arange(ngrp, dtype=jnp.int32)[None, :]
    byte = lambda v: jnp.stack([v // BYTE, v % BYTE], axis=-1)
    table = jnp.concatenate([local, byte(grp_before), jnp.broadcast_to(byte(gidx), (n_exp, ngrp, 2))],
                            axis=-1).astype(BF16)
    slots = jnp.arange(cap_p, dtype=jnp.int32)[None, :, None]
    in_grp = ((grp_before[:, None, :] <= slots) & (slots < grp_end[:, None, :])).astype(BF16)
    hit = jnp.einsum("esg,egc->esc", in_grp, table, preferred_element_type=F32).astype(jnp.int32)
    j = slots[:, :, 0] - (hit[:, :, LANES] * BYTE + hit[:, :, LANES + 1])
    lane = jnp.sum(hit[:, :, :LANES] <= j[:, :, None], axis=-1, dtype=jnp.int32)
    rows = (hit[:, :, LANES + 2] * BYTE + hit[:, :, LANES + 3]) * LANES + lane
    rows = jnp.where(slots[:, :, 0] < cap, rows, 0)
    return rows, slot, gate, start_al, rounds.reshape(-1)


def _project(x, meta_blk, prm, cast_srcs=()):
    b, s, d = x.shape
    assert s % TM == 0 and d % LANES == 0
    cos2, sin2 = _rope_tables(s)
    return _stage_a(x, meta_blk, cos2, sin2, prm["gmix"], prm["gq2"], prm["gk2"], prm["bd"],
                    prm["w_in"], cast_srcs)


def _encoder(x, meta_blk, prm, projected, expert_w):
    b, s, d = x.shape
    li = s + BLK
    q, k, v, cb, sprod, ga, gc = projected
    attn = _stage_b(q, k, v, prm["sink_rows"])
    x1, h2, aff_t = _stage_c(x, meta_blk, attn, cb, sprod, ga, gc, prm["wconv"], prm["gffn"],
                             prm["wa"], prm["wc"], prm["wo"], prm["wr_t"])
    rows, slot, gate, start_al, rounds = _route(aff_t, s)
    xs = h2.reshape(b * li, d)[rows]
    ye = _stage_d(xs, *expert_w)
    return _stage_e(x1, slot, gate, ye, start_al, rounds, s)


def kernel(x_prompt, x_sample, meta_tokens, g_mix, w_in, g_q, g_k, sink_logits, w_conv, w_attn_out,
           w_conv_out, w_out, g_ffn, w_router, w_expert_gate, w_expert_up, w_expert_down):
    assert w_in.shape[0] == 1, "single-layer block"
    d = x_prompt.shape[-1]
    qw = N_HEADS * HEAD_DIM
    perm = _q_head_perm()
    w_in0 = w_in[0]
    lane = np.arange(LANES)
    prm = {
        "w_in": jnp.concatenate([w_in0[:, :qw][:, perm], w_in0[:, qw:]], axis=1).astype(BF16),
        "gmix": g_mix[0][None, :].astype(F32),
        "gq2": jnp.tile(g_q[0], LANES // HEAD_DIM)[None, :].astype(F32),
        "gk2": jnp.tile(g_k[0], LANES // HEAD_DIM)[None, :].astype(F32),
        "bd": jnp.asarray((lane[:, None] // HEAD_DIM) == (lane[None, :] // HEAD_DIM), BF16),
        "sink_rows": _sink_rows(sink_logits[0]),
        "wconv": w_conv[0].astype(F32),
        "gffn": g_ffn[0][None, :].astype(F32),
        "wa": w_attn_out[0][perm, :].astype(BF16),
        "wc": w_conv_out[0].astype(BF16),
        "wo": w_out[0].astype(BF16),
        "wr_t": jnp.transpose(w_router[0]).astype(BF16),
    }
    meta_blk = jnp.concatenate([jnp.zeros((META_ROW0, d), F32), meta_tokens.astype(F32)], axis=0)
    expert_w = [w_expert_gate[0], w_expert_up[0], w_expert_down[0]]
    flat = [w.astype(F32).reshape(-1, w.shape[-1]) for w in expert_w]
    steps = x_prompt.shape[0] * (x_prompt.shape[1] // TM)
    if all(_cast_rows(w, steps) for w in flat):
        proj_prompt, cast = _project(x_prompt, meta_blk, prm, flat)
        expert_w = [c.reshape(w.shape) for c, w in zip(cast, expert_w)]
    else:
        proj_prompt, _ = _project(x_prompt, meta_blk, prm)
        expert_w = [w.astype(BF16) for w in expert_w]
    proj_sample, _ = _project(x_sample, meta_blk, prm)
    return (_encoder(x_prompt, meta_blk, prm, proj_prompt, expert_w),
            _encoder(x_sample, meta_blk, prm, proj_sample, expert_w))
```

```python
import functools

import numpy as np
import jax
import jax.numpy as jnp
from jax import lax
from jax.experimental import pallas as pl
from jax.experimental.pallas import tpu as pltpu

N_META = 16
N_HEADS = 16
N_KV_HEADS = 4
HEAD_DIM = 64
CAPACITY_FACTOR = 2
ROPE_THETA = 10000.0
EPS = 1e-6
NEG = -1e30

LANES = 128
BLK = 128
META_ROW0 = BLK - N_META
TM = 512
TE = 1056
HALO = 16
DEN_ROWS = 16
ATTN_LOOKAHEAD = 4
ATTN_BLOCKS = 3
CH = 128
UNSELECTED = -(1 << 30)
NO_LANE = 1 << 30
BYTE = 256
LOG2E = 1.4426950408889634
Q_SCALE = HEAD_DIM ** -0.5 * LOG2E
VMEM_LIMIT = 56 * 1024 * 1024
CAST_CHUNK_BYTES = 2 * 1024 * 1024

F32 = jnp.float32
BF16 = jnp.bfloat16


def _dot(a, b):
    return jnp.dot(a, b, preferred_element_type=F32)


def _dot_nt(a, b):
    return lax.dot_general(a, b, (((1,), (1,)), ((), ())), preferred_element_type=F32)


def _cparams(n_axes):
    return pltpu.CompilerParams(dimension_semantics=("arbitrary",) * n_axes,
                                vmem_limit_bytes=VMEM_LIMIT)


def _proj_kernel(x_ref, meta_ref, cos_ref, sin_ref, gmix_ref, gq_ref, gk_ref, bd_ref, w_ref, *refs,
                 nreal, d, qw, kvw, cw, n_cast):
    cast_in, refs = refs[:n_cast], refs[n_cast:]
    (q_ref, k_ref, v_ref, cb_ref, s_ref, ga_ref, gc_ref), cast_out = refs[:7], refs[7:]
    j = pl.program_id(1)
    lane = lax.broadcasted_iota(jnp.int32, (1, LANES), 1)
    first_half = jnp.bitwise_and(lane, HEAD_DIM - 1) < (HEAD_DIM // 2)

    def norm_rope(t, ss, g, cos, sin):
        t = t * lax.rsqrt(ss * (1.0 / HEAD_DIM) + EPS) * g
        partner = jnp.where(first_half, pltpu.roll(t, LANES - HEAD_DIM // 2, 1),
                            pltpu.roll(t, HEAD_DIM // 2, 1))
        return t * cos + partner * sin

    def body(x, cos, sin, rows):
        ms = jnp.mean(x * x, axis=-1, keepdims=True)
        h = (x * lax.rsqrt(ms + EPS) * gmix_ref[...]).astype(BF16)
        qk = _dot(h, w_ref[:, :qw + kvw])
        off = qw + kvw
        v_ref[0, :rows, :] = _dot(h, w_ref[:, off:off + kvw]).astype(BF16)
        off += kvw
        chunks = [qk[:, c * LANES:(c + 1) * LANES] for c in range((qw + kvw) // LANES)]
        sumsq = [_dot((t * t).astype(BF16), bd_ref[...]) for t in chunks]
        cb_ref[0, :rows, :] = _dot(h, w_ref[:, off:off + cw]).astype(BF16)
        off += cw
        cc = _dot(h, w_ref[:, off:off + cw])
        off += cw
        cu = _dot(h, w_ref[:, off:off + cw])
        off += cw
        s_ref[0, :rows, :] = (cc * cu).astype(BF16)
        ga_ref[0, :rows, :] = jax.nn.sigmoid(_dot(h, w_ref[:, off:off + d])).astype(BF16)
        off += d
        gc_ref[0, :rows, :] = jax.nn.sigmoid(_dot(h, w_ref[:, off:off + d])).astype(BF16)
        for c, (t, ss) in enumerate(zip(chunks, sumsq)):
            is_q = c < qw // LANES
            t = norm_rope(t, ss, (gq_ref if is_q else gk_ref)[...], cos, sin)
            if is_q:
                q_ref[0, :rows, c * LANES:(c + 1) * LANES] = (t * Q_SCALE).astype(BF16)
            else:
                ck = c - qw // LANES
                k_ref[0, :rows, ck * LANES:(ck + 1) * LANES] = t.astype(BF16)

    @pl.when(j < nreal)
    def _():
        body(x_ref[0], cos_ref[...], sin_ref[...], TM)
        for src, dst in zip(cast_in, cast_out):
            dst[...] = src[...].astype(BF16)

    @pl.when(j == nreal)
    def _():
        body(meta_ref[...], cos_ref[:BLK, :], sin_ref[:BLK, :], BLK)


def _cast_rows(arr, steps):
    rows, cols = arr.shape
    r = rows // steps
    ok = rows % steps == 0 and r % HALO == 0 and r * cols * 4 <= CAST_CHUNK_BYTES
    return r if ok else 0


def _stage_a(x, meta_blk, cos2, sin2, gmix, gq2, gk2, bd, w_in_b, cast_srcs=()):
    b, s, d = x.shape
    li = s + BLK
    nreal = s // TM
    qw, kvw, cw = N_HEADS * HEAD_DIM, N_KV_HEADS * HEAD_DIM, d
    const = lambda bi, j: (0, 0)
    rowblk = lambda bi, j: (bi, j, 0)
    out_w = (qw, kvw, kvw, cw, cw, d, d)
    chunk = lambda bi, j: (bi * nreal + jnp.minimum(j, nreal - 1), 0)
    cast_specs = [pl.BlockSpec((_cast_rows(a, b * nreal), a.shape[1]), chunk) for a in cast_srcs]
    outs = pl.pallas_call(
        functools.partial(_proj_kernel, nreal=nreal, d=d, qw=qw, kvw=kvw, cw=cw, n_cast=len(cast_srcs)),
        grid=(b, nreal + 1),
        in_specs=[
            pl.BlockSpec((1, TM, d), lambda bi, j: (bi, jnp.minimum(j, nreal - 1), 0)),
            pl.BlockSpec((BLK, d), const),
            pl.BlockSpec((TM, LANES), lambda bi, j: (j, 0)),
            pl.BlockSpec((TM, LANES), lambda bi, j: (j, 0)),
            pl.BlockSpec((1, d), const),
            pl.BlockSpec((1, LANES), const),
            pl.BlockSpec((1, LANES), const),
            pl.BlockSpec((LANES, LANES), const),
            pl.BlockSpec(w_in_b.shape, const, pipeline_mode=pl.Buffered(1)),
        ] + cast_specs,
        out_specs=[pl.BlockSpec((1, TM, w), rowblk) for w in out_w] + cast_specs,
        out_shape=[jax.ShapeDtypeStruct((b, li, w), BF16) for w in out_w]
        + [jax.ShapeDtypeStruct(a.shape, BF16) for a in cast_srcs],
        compiler_params=_cparams(2),
        name="proj_stage",
    )(x, meta_blk, cos2, sin2, gmix, gq2, gk2, bd, w_in_b, *cast_srcs)
    return outs[:7], outs[7:]


def _attn_kernel(*refs, nblk):
    nq, nkv = ATTN_BLOCKS, ATTN_BLOCKS + 2
    q_refs, k_refs, v_refs = refs[:nq], refs[nq:nq + nkv], refs[nq + nkv:nq + 2 * nkv]
    sink_ref, o_ref = refs[nq + 2 * nkv:]
    kj = lax.broadcasted_iota(jnp.int32, (3 * BLK, BLK), 0)
    qi = lax.broadcasted_iota(jnp.int32, (3 * BLK, BLK), 1)
    band = (kj >= qi) & (kj <= qi + 2 * BLK)
    seg = jnp.right_shift(kj, BLK.bit_length() - 1)
    row = jnp.bitwise_and(kj, BLK - 1)
    lane = lax.broadcasted_iota(jnp.int32, (1, LANES), 1)
    first = lane < HEAD_DIM
    first_rows = lax.broadcasted_iota(jnp.int32, (LANES, 1), 0) < HEAD_DIM
    zero = jnp.zeros((), BF16)
    groups = N_HEADS // N_KV_HEADS
    n_pairs = N_KV_HEADS * HEAD_DIM // LANES
    ok2, kcat, v_t = [], [], []
    for u in range(nq):
        sb = pl.program_id(1) * nq + u
        lo_prev = jnp.where(sb == 0, META_ROW0, jnp.where(sb == nblk - 1, BLK, 0))
        lo_cur = jnp.where(sb == nblk - 1, META_ROW0, 0)
        lo_next = jnp.where(sb == nblk - 2, BLK, 0)
        lo = jnp.where(seg == 0, lo_prev, jnp.where(seg == 1, lo_cur, lo_next))
        ok = band & (row >= lo)
        ok2.append(jnp.concatenate([ok, ok], axis=1))
        for m in range(n_pairs):
            cols = slice(m * LANES, (m + 1) * LANES)
            kcat.append(jnp.concatenate([r[0, :, cols] for r in k_refs[u:u + 3]], axis=0))
            vcat = jnp.concatenate([r[0, :, cols] for r in v_refs[u:u + 3]], axis=0)
            v_t.append(jnp.concatenate([jnp.transpose(vcat.astype(F32)).astype(BF16),
                                        jnp.ones((DEN_ROWS, 3 * BLK), BF16)], axis=0))

    per_blk = n_pairs * groups

    def scores(n):
        u, c = divmod(n, per_blk)
        qc = q_refs[u][0, :, c * LANES:(c + 1) * LANES]
        lhs = jnp.concatenate([jnp.where(first, qc, zero), jnp.where(first, zero, qc)], axis=0)
        return _dot_nt(kcat[u * n_pairs + c // groups], lhs)

    n_chunks = nq * per_blk
    pending = [scores(n) for n in range(ATTN_LOOKAHEAD)]
    for n in range(n_chunks):
        u, c = divmod(n, per_blk)
        m, i = divmod(c, groups)
        if n + ATTN_LOOKAHEAD < n_chunks:
            pending.append(scores(n + ATTN_LOOKAHEAD))
        s_t = jnp.where(ok2[u], pending.pop(0), NEG)
        sink = sink_ref[m, :, 2 * i * BLK:(2 * i + 2) * BLK]
        mx = jnp.maximum(jnp.max(s_t, axis=0, keepdims=True), sink)
        p = jnp.exp2(s_t - mx)
        o_t = _dot(v_t[u * n_pairs + m], p.astype(BF16))
        den = o_t[LANES:LANES + 1, :] + jnp.exp2(sink - mx)
        o_t = o_t[:LANES, :] / den
        blk = jnp.where(first_rows, o_t[:, :BLK], o_t[:, BLK:])
        o_ref[0, u * BLK:(u + 1) * BLK, c * LANES:(c + 1) * LANES] = jnp.transpose(blk).astype(BF16)


def _stage_b(q, k, v, sink_rows):
    b, li, qw = q.shape
    kvw = k.shape[-1]
    nblk = li // BLK
    nq = ATTN_BLOCKS
    q_specs = [pl.BlockSpec((1, BLK, qw), lambda bi, j, u=u: (bi, jnp.minimum(j * nq + u, nblk - 1), 0))
               for u in range(nq)]
    kv_specs = [pl.BlockSpec((1, BLK, kvw), lambda bi, j, u=u: (bi, (j * nq + u + nblk - 1) % nblk, 0))
                for u in range(nq + 2)]
    return pl.pallas_call(
        functools.partial(_attn_kernel, nblk=nblk),
        grid=(b, -(-nblk // nq)),
        in_specs=q_specs + kv_specs + kv_specs
        + [pl.BlockSpec(sink_rows.shape, lambda bi, j: (0, 0, 0))],
        out_specs=pl.BlockSpec((1, nq * BLK, qw), lambda bi, j: (bi, j, 0)),
        out_shape=jax.ShapeDtypeStruct((b, li, qw), BF16),
        compiler_params=_cparams(2),
        name="attn_stage",
    )(*([q] * nq + [k] * (nq + 2) + [v] * (nq + 2) + [sink_rows]))


def _mix_kernel(x_ref, meta_ref, attn_ref, cb_ref, s_ref, sp_ref, sn_ref, ga_ref, gc_ref,
                wconv_ref, gffn_ref, wa_ref, wc_ref, wo_ref, wr_ref,
                x1_ref, h2_ref, aff_ref, *, nreal):
    j = pl.program_id(1)

    def body(x, rows, n_pad):
        attn_p = _dot(attn_ref[0, :rows, :], wa_ref[...])
        s = s_ref[0, :rows, :].astype(F32)
        ridx = lax.broadcasted_iota(jnp.int32, (rows, 1), 0)
        s_prev = jnp.where(ridx == 0, sp_ref[0, HALO - 1:HALO, :].astype(F32), pltpu.roll(s, 1, 0))
        s_next = jnp.where(ridx == rows - 1, sn_ref[0, 0:1, :].astype(F32), pltpu.roll(s, rows - 1, 0))
        conv = wconv_ref[0:1, :] * s_prev + wconv_ref[1:2, :] * s + wconv_ref[2:3, :] * s_next
        conv_in = (cb_ref[0, :rows, :].astype(F32) * conv).astype(BF16)
        conv_p = _dot(conv_in, wc_ref[...])
        merged = ga_ref[0, :rows, :].astype(F32) * attn_p + gc_ref[0, :rows, :].astype(F32) * conv_p
        x1 = x + _dot(merged.astype(BF16), wo_ref[...])
        x1_ref[0, :rows, :] = x1
        ms = jnp.mean(x1 * x1, axis=-1, keepdims=True)
        h2 = (x1 * lax.rsqrt(ms + EPS) * gffn_ref[...]).astype(BF16)
        h2_ref[0, :rows, :] = h2
        logits = _dot_nt(wr_ref[...], h2)
        e = jnp.exp(logits - jnp.max(logits, axis=0, keepdims=True))
        aff = e / jnp.sum(e, axis=0, keepdims=True)
        if n_pad:
            aff = jnp.where(lax.broadcasted_iota(jnp.int32, aff.shape, 1) < n_pad, -1.0, aff)
        aff_ref[0, :, :rows] = aff

    @pl.when(j < nreal)
    def _():
        body(x_ref[0], TM, 0)

    @pl.when(j == nreal)
    def _():
        body(meta_ref[...], BLK, META_ROW0)


def _stage_c(x, meta_blk, attn, cb, sprod, ga, gc, wconv, gffn, wa, wc, wo, wr_t):
    b, s, d = x.shape
    li = s + BLK
    nreal = s // TM
    nhalo = li // HALO
    const = lambda bi, j: (0, 0)
    rowblk = lambda bi, j: (bi, j, 0)
    halo_prev = lambda bi, j: (bi, (j * (TM // HALO) + nhalo - 1) % nhalo, 0)
    halo_next = lambda bi, j: (bi, jnp.where(j == nreal, 0, (j + 1) * (TM // HALO)), 0)
    n_exp = wr_t.shape[0]
    wspec = lambda w: pl.BlockSpec(w.shape, const, pipeline_mode=pl.Buffered(1))
    return pl.pallas_call(
        functools.partial(_mix_kernel, nreal=nreal),
        grid=(b, nreal + 1),
        in_specs=[
            pl.BlockSpec((1, TM, d), lambda bi, j: (bi, jnp.minimum(j, nreal - 1), 0)),
            pl.BlockSpec((BLK, d), const),
            pl.BlockSpec((1, TM, d), rowblk),
            pl.BlockSpec((1, TM, d), rowblk),
            pl.BlockSpec((1, TM, d), rowblk),
            pl.BlockSpec((1, HALO, d), halo_prev),
            pl.BlockSpec((1, HALO, d), halo_next),
            pl.BlockSpec((1, TM, d), rowblk),
            pl.BlockSpec((1, TM, d), rowblk),
            pl.BlockSpec(wconv.shape, const),
            pl.BlockSpec((1, d), const),
            wspec(wa), wspec(wc), wspec(wo), wspec(wr_t),
        ],
        out_specs=[
            pl.BlockSpec((1, TM, d), rowblk),
            pl.BlockSpec((1, TM, d), rowblk),
            pl.BlockSpec((1, n_exp, TM), lambda bi, j: (bi, 0, j)),
        ],
        out_shape=[
            jax.ShapeDtypeStruct((b, li, d), F32),
            jax.ShapeDtypeStruct((b, li, d), BF16),
            jax.ShapeDtypeStruct((b, n_exp, li), F32),
        ],
        compiler_params=_cparams(2),
        name="mix_stage",
    )(x, meta_blk, attn, cb, sprod, sprod, sprod, ga, gc, wconv, gffn, wa, wc, wo, wr_t)


def _expert_kernel(xs_ref, wg_ref, wu_ref, wd_ref, y_ref):
    x = xs_ref[0]
    g = _dot(x, wg_ref[0])
    u = _dot(x, wu_ref[0])
    he = (g * jax.nn.sigmoid(g) * u).astype(BF16)
    y_ref[0] = _dot(he, wd_ref[0]).astype(y_ref.dtype)


def _stage_d(xs, wg, wu, wd):
    e, cap_p, d = xs.shape
    de = wg.shape[-1]
    tile = lambda ei, t: (ei, t, 0)
    wmap = lambda ei, t: (ei, 0, 0)
    return pl.pallas_call(
        _expert_kernel,
        grid=(e, cap_p // TE),
        in_specs=[
            pl.BlockSpec((1, TE, d), tile),
            pl.BlockSpec((1, d, de), wmap),
            pl.BlockSpec((1, d, de), wmap),
            pl.BlockSpec((1, de, d), wmap),
        ],
        out_specs=pl.BlockSpec((1, TE, d), tile),
        out_shape=jax.ShapeDtypeStruct((e, cap_p, d), BF16),
        compiler_params=_cparams(2),
        name="expert_stage",
    )(xs, wg, wu, wd)


def _combine_kernel(start_ref, rounds_ref, x1_ref, slot_ref, gate_ref, ye_ref, o_ref, ybuf, sem, *,
                    n_exp, nsteps):
    lin = pl.program_id(0) * pl.num_programs(1) + pl.program_id(1)

    def chunk_start(step, e, rnd):
        return jnp.minimum(start_ref[step * n_exp + e] + rnd * CH, ye_ref.shape[1] - CH)

    def chunk_copy(step, e, rnd, buf):
        st = pl.multiple_of(chunk_start(step, e, rnd), HALO)
        return pltpu.make_async_copy(ye_ref.at[e, pl.ds(st, CH), :],
                                     ybuf.at[buf, pl.ds(e * CH, CH), :], sem.at[buf])

    def fetch(step, rnd, buf):
        for e in range(n_exp):
            chunk_copy(step, e, rnd, buf).start()

    def wait(step, rnd, buf):
        for e in range(n_exp):
            chunk_copy(step, e, rnd, buf).wait()

    buf = lin % 2

    @pl.when(lin == 0)
    def _():
        fetch(0, 0, 0)

    @pl.when(lin + 1 < nsteps)
    def _():
        fetch(lin + 1, 0, 1 - buf)

    slot = jnp.transpose(slot_ref[0])
    gate = jnp.transpose(gate_ref[0])
    lane_c = lax.broadcasted_iota(jnp.int32, (1, CH), 1)

    def gathered(rnd):
        perm = []
        for e in range(n_exp):
            st = chunk_start(lin, e, rnd)
            first_lane = start_ref[lin * n_exp + e] + rnd * CH - st
            lanes = jnp.where(lane_c >= first_lane, lane_c, NO_LANE)
            hit = slot[:, e:e + 1] - st == lanes
            perm.append(jnp.where(hit, gate[:, e:e + 1], 0.0).astype(BF16))
        return _dot(jnp.concatenate(perm, axis=1), ybuf[buf])

    wait(lin, 0, buf)
    o_ref[0] = x1_ref[0] + gathered(0)

    def extra(rnd, carry):
        fetch(lin, rnd, buf)
        wait(lin, rnd, buf)
        o_ref[0] += gathered(rnd)
        return carry

    lax.fori_loop(1, rounds_ref[lin], extra, 0)


def _stage_e(x1, slot, gate, ye, start_al, rounds, s):
    b, li, d = x1.shape
    n_exp = slot.shape[1]
    ntile = s // TM
    tile = lambda bi, j, *_: (bi, j, 0)
    return pl.pallas_call(
        functools.partial(_combine_kernel, n_exp=n_exp, nsteps=b * ntile),
        grid_spec=pltpu.PrefetchScalarGridSpec(
            num_scalar_prefetch=2,
            grid=(b, ntile),
            in_specs=[
                pl.BlockSpec((1, TM, d), tile),
                pl.BlockSpec((1, n_exp, TM), lambda bi, j, *_: (bi, 0, j)),
                pl.BlockSpec((1, n_exp, TM), lambda bi, j, *_: (bi, 0, j)),
                pl.BlockSpec(memory_space=pl.ANY),
            ],
            out_specs=pl.BlockSpec((1, TM, d), tile),
            scratch_shapes=[pltpu.VMEM((2, n_exp * CH, d), BF16), pltpu.SemaphoreType.DMA((2,))],
        ),
        out_shape=jax.ShapeDtypeStruct((b, s, d), F32),
        compiler_params=_cparams(2),
        name="combine_stage",
    )(start_al, rounds, x1, slot, gate, ye)


def _q_head_perm():
    groups = N_HEADS // N_KV_HEADS
    cols = []
    for m in range(N_KV_HEADS // 2):
        for i in range(groups):
            for h in (2 * groups * m + i, 2 * groups * m + groups + i):
                cols.extend(range(h * HEAD_DIM, (h + 1) * HEAD_DIM))
    return np.asarray(cols, np.int32)


def _rope_tables(s):
    pos = jnp.concatenate([N_META + jnp.arange(s, dtype=F32), jnp.zeros((META_ROW0,), F32),
                           jnp.arange(N_META, dtype=F32)])
    inv = ROPE_THETA ** (-jnp.arange(0, HEAD_DIM, 2, dtype=F32) / HEAD_DIM)
    ang = pos[:, None] * inv[None, :]
    reps = LANES // (HEAD_DIM // 2)
    cos2 = jnp.tile(jnp.cos(ang), (1, reps))
    sgn = jnp.where((jnp.arange(LANES) % HEAD_DIM) < HEAD_DIM // 2, -1.0, 1.0).astype(F32)
    sin2 = jnp.tile(jnp.sin(ang), (1, reps)) * sgn[None, :]
    return cos2, sin2


def _sink_rows(sink):
    groups = N_HEADS // N_KV_HEADS
    heads = np.asarray([[2 * groups * m + groups * half + i
                         for i in range(groups) for half in range(2)]
                        for m in range(N_KV_HEADS // 2)], np.int32)
    rows = jnp.repeat(sink.astype(F32)[heads] * LOG2E, BLK, axis=1)
    return rows[:, None, :]


def _threshold_kernel(aff_ref, thr_ref, *, cap):
    bits = pltpu.bitcast(aff_ref[...], jnp.int32)

    def step(i, prefix):
        cand = prefix | jnp.left_shift(jnp.int32(1), 30 - i)
        count = jnp.sum((bits >= cand[None]).astype(jnp.int32), axis=2, keepdims=True)
        return jnp.where(jnp.sum(count, axis=0) >= cap, cand, prefix)

    prefix = lax.fori_loop(0, 31, step, jnp.zeros((bits.shape[1], 1), jnp.int32))
    thr_ref[...] = jnp.broadcast_to(pltpu.bitcast(prefix, F32), thr_ref.shape)


def _thresholds(aff_t, cap):
    n_exp = aff_t.shape[1]
    out = pl.pallas_call(
        functools.partial(_threshold_kernel, cap=cap),
        out_shape=jax.ShapeDtypeStruct((n_exp, LANES), F32),
        compiler_params=pltpu.CompilerParams(vmem_limit_bytes=VMEM_LIMIT),
        name="threshold_stage",
    )(aff_t)
    return out[:, 0]


def _prefix_counts(mask):
    b, e, li = mask.shape
    g = li // LANES
    tri_l = jnp.asarray(np.tri(LANES, dtype=np.float32).T, BF16)
    tri_g = jnp.asarray(np.tri(b * g, k=-1, dtype=np.float32).T, BF16)
    local = jnp.einsum("begl,lm->begm", mask.reshape(b, e, g, LANES).astype(BF16), tri_l,
                       preferred_element_type=F32)
    totals = jnp.transpose(local[:, :, :, LANES - 1], (1, 0, 2)).reshape(e, b * g)
    before = jnp.einsum("eg,gh->eh", totals.astype(BF16), tri_g, preferred_element_type=F32)
    before = jnp.transpose(before.reshape(e, b, g), (1, 0, 2))
    return local.astype(jnp.int32), before.astype(jnp.int32)


def _route(aff, s):
    b, n_exp, li = aff.shape
    grp_seq = li // LANES
    ngrp = b * grp_seq
    n_tok = b * (s + N_META)
    cap = max(1, CAPACITY_FACTOR * n_tok // n_exp)
    cap_p = max(-(-cap // TE) * TE, CH)
    thr = _thresholds(aff, cap)[None, :, None]
    above = aff > thr
    equal = aff == thr
    room = cap - jnp.sum(above, axis=(0, 2), dtype=jnp.int32)[None, :, None]

    def earliest_ties(_):
        local, before = _prefix_counts(equal)
        eq_upto = (local + before[:, :, :, None]).reshape(b, n_exp, li)
        eq_real_end = eq_upto[:, :, s - 1:s]
        eq_before_seq = eq_upto[:, :, 0:1] - equal[:, :, 0:1]
        eq_real = eq_real_end - eq_before_seq
        eq_meta = eq_upto[:, :, li - 1:li] - eq_real_end
        is_real = (jnp.arange(li) < s)[None, None, :]
        rank = eq_upto + jnp.where(is_real, eq_meta, -eq_real)
        return above | (equal & (rank <= room))

    all_ties_fit = jnp.all(jnp.sum(equal, axis=(0, 2), dtype=jnp.int32)[None, :, None] == room)
    sel = lax.cond(all_ties_fit, lambda _: above | equal, earliest_ties, None)

    local, grp_before = _prefix_counts(sel)
    upto = (local + grp_before[:, :, :, None]).reshape(b, n_exp, li)
    slot = jnp.where(sel, upto - 1, UNSELECTED)
    gate = jnp.where(sel, aff, 0.0)

    tile_grp = TM // LANES
    start = grp_before[:, :, 0:s // LANES:tile_grp]
    end = grp_before[:, :, tile_grp:s // LANES + 1:tile_grp]
    start_al = start // HALO * HALO
    rounds = jnp.maximum(jnp.max(-(-(end - start_al) // CH), axis=1), 1)
    start_al = jnp.transpose(start_al, (0, 2, 1)).reshape(-1)

    local = jnp.transpose(local, (1, 0, 2, 3)).reshape(n_exp, ngrp, LANES)
    grp_before = jnp.transpose(grp_before, (1, 0, 2)).reshape(n_exp, ngrp)
    grp_end = jnp.concatenate([grp_before[:, 1:], jnp.full((n_exp, 1), cap, jnp.int32)], axis=1)
    gidx = jnp.arange(ngrp, dtype=jnp.int32)[None, :]
    byte = lambda v: jnp.stack([v // BYTE, v % BYTE], axis=-1)
    table = jnp.concatenate([local, byte(grp_before), jnp.broadcast_to(byte(gidx), (n_exp, ngrp, 2))],
                            axis=-1).astype(BF16)
    slots = jnp.arange(cap_p, dtype=jnp.int32)[None, :, None]
    in_grp = ((grp_before[:, None, :] <= slots) & (slots < grp_end[:, None, :])).astype(BF16)
    hit = jnp.einsum("esg,egc->esc", in_grp, table, preferred_element_type=F32).astype(jnp.int32)
    j = slots[:, :, 0] - (hit[:, :, LANES] * BYTE + hit[:, :, LANES + 1])
    lane = jnp.sum(hit[:, :, :LANES] <= j[:, :, None], axis=-1, dtype=jnp.int32)
    rows = (hit[:, :, LANES + 2] * BYTE + hit[:, :, LANES + 3]) * LANES + lane
    rows = jnp.where(slots[:, :, 0] < cap, rows, 0)
    return rows, slot, gate, start_al, rounds.reshape(-1)


def _project(x, meta_blk, prm, cast_srcs=()):
    b, s, d = x.shape
    assert s % TM == 0 and d % LANES == 0
    cos2, sin2 = _rope_tables(s)
    return _stage_a(x, meta_blk, cos2, sin2, prm["gmix"], prm["gq2"], prm["gk2"], prm["bd"],
                    prm["w_in"], cast_srcs)


def _encoder(x, meta_blk, prm, projected, expert_w):
    b, s, d = x.shape
    li = s + BLK
    q, k, v, cb, sprod, ga, gc = projected
    attn = _stage_b(q, k, v, prm["sink_rows"])
    x1, h2, aff_t = _stage_c(x, meta_blk, attn, cb, sprod, ga, gc, prm["wconv"], prm["gffn"],
                             prm["wa"], prm["wc"], prm["wo"], prm["wr_t"])
    rows, slot, gate, start_al, rounds = _route(aff_t, s)
    xs = h2.reshape(b * li, d)[rows]
    ye = _stage_d(xs, *expert_w)
    return _stage_e(x1, slot, gate, ye, start_al, rounds, s)


def kernel(x_prompt, x_sample, meta_tokens, g_mix, w_in, g_q, g_k, sink_logits, w_conv, w_attn_out,
           w_conv_out, w_out, g_ffn, w_router, w_expert_gate, w_expert_up, w_expert_down):
    assert w_in.shape[0] == 1, "single-layer block"
    d = x_prompt.shape[-1]
    qw = N_HEADS * HEAD_DIM
    perm = _q_head_perm()
    w_in0 = w_in[0]
    lane = np.arange(LANES)
    prm = {
        "w_in": jnp.concatenate([w_in0[:, :qw][:, perm], w_in0[:, qw:]], axis=1).astype(BF16),
        "gmix": g_mix[0][None, :].astype(F32),
        "gq2": jnp.tile(g_q[0], LANES // HEAD_DIM)[None, :].astype(F32),
        "gk2": jnp.tile(g_k[0], LANES // HEAD_DIM)[None, :].astype(F32),
        "bd": jnp.asarray((lane[:, None] // HEAD_DIM) == (lane[None, :] // HEAD_DIM), BF16),
        "sink_rows": _sink_rows(sink_logits[0]),
        "wconv": w_conv[0].astype(F32),
        "gffn": g_ffn[0][None, :].astype(F32),
        "wa": w_attn_out[0][perm, :].astype(BF16),
        "wc": w_conv_out[0].astype(BF16),
        "wo": w_out[0].astype(BF16),
        "wr_t": jnp.transpose(w_router[0]).astype(BF16),
    }
    meta_blk = jnp.concatenate([jnp.zeros((META_ROW0, d), F32), meta_tokens.astype(F32)], axis=0)
    expert_w = [w_expert_gate[0], w_expert_up[0], w_expert_down[0]]
    flat = [w.astype(F32).reshape(-1, w.shape[-1]) for w in expert_w]
    steps = x_prompt.shape[0] * (x_prompt.shape[1] // TM)
    if all(_cast_rows(w, steps) for w in flat):
        proj_prompt, cast = _project(x_prompt, meta_blk, prm, flat)
        expert_w = [c.reshape(w.shape) for c, w in zip(cast, expert_w)]
    else:
        proj_prompt, _ = _project(x_prompt, meta_blk, prm)
        expert_w = [w.astype(BF16) for w in expert_w]
    proj_sample, _ = _project(x_sample, meta_blk, prm)
    return (_encoder(x_prompt, meta_blk, prm, proj_prompt, expert_w),
            _encoder(x_sample, meta_blk, prm, proj_sample, expert_w))
```

```python
import functools

import numpy as np
import jax
import jax.numpy as jnp
from jax import lax
from jax.experimental import pallas as pl
from jax.experimental.pallas import tpu as pltpu

N_META = 16
N_HEADS = 16
N_KV_HEADS = 4
HEAD_DIM = 64
CAPACITY_FACTOR = 2
ROPE_THETA = 10000.0
EPS = 1e-6
NEG = -1e30

LANES = 128
BLK = 128
META_ROW0 = BLK - N_META
TM = 512
TE = 1056
HALO = 16
DEN_ROWS = 16
ATTN_LOOKAHEAD = 4
ATTN_BLOCKS = 3
CH = 128
UNSELECTED = -(1 << 30)
BYTE = 256
LOG2E = 1.4426950408889634
Q_SCALE = HEAD_DIM ** -0.5 * LOG2E
VMEM_LIMIT = 56 * 1024 * 1024
CAST_CHUNK_BYTES = 2 * 1024 * 1024

F32 = jnp.float32
BF16 = jnp.bfloat16


def _dot(a, b):
    return jnp.dot(a, b, preferred_element_type=F32)


def _dot_nt(a, b):
    return lax.dot_general(a, b, (((1,), (1,)), ((), ())), preferred_element_type=F32)


def _cparams(n_axes):
    return pltpu.CompilerParams(dimension_semantics=("arbitrary",) * n_axes,
                                vmem_limit_bytes=VMEM_LIMIT)


def _proj_kernel(x_ref, meta_ref, cos_ref, sin_ref, gmix_ref, gq_ref, gk_ref, bd_ref, w_ref, *refs,
                 nreal, d, qw, kvw, cw, n_cast):
    cast_in, refs = refs[:n_cast], refs[n_cast:]
    (q_ref, k_ref, v_ref, cb_ref, s_ref, ga_ref, gc_ref), cast_out = refs[:7], refs[7:]
    j = pl.program_id(1)
    lane = lax.broadcasted_iota(jnp.int32, (1, LANES), 1)
    first_half = jnp.bitwise_and(lane, HEAD_DIM - 1) < (HEAD_DIM // 2)

    def norm_rope(t, ss, g, cos, sin):
        t = t * lax.rsqrt(ss * (1.0 / HEAD_DIM) + EPS) * g
        partner = jnp.where(first_half, pltpu.roll(t, LANES - HEAD_DIM // 2, 1),
                            pltpu.roll(t, HEAD_DIM // 2, 1))
        return t * cos + partner * sin

    def body(x, cos, sin, rows):
        ms = jnp.mean(x * x, axis=-1, keepdims=True)
        h = (x * lax.rsqrt(ms + EPS) * gmix_ref[...]).astype(BF16)
        qk = _dot(h, w_ref[:, :qw + kvw])
        off = qw + kvw
        v_ref[0, :rows, :] = _dot(h, w_ref[:, off:off + kvw]).astype(BF16)
        off += kvw
        chunks = [qk[:, c * LANES:(c + 1) * LANES] for c in range((qw + kvw) // LANES)]
        sumsq = [_dot((t * t).astype(BF16), bd_ref[...]) for t in chunks]
        cb_ref[0, :rows, :] = _dot(h, w_ref[:, off:off + cw]).astype(BF16)
        off += cw
        cc = _dot(h, w_ref[:, off:off + cw])
        off += cw
        cu = _dot(h, w_ref[:, off:off + cw])
        off += cw
        s_ref[0, :rows, :] = (cc * cu).astype(BF16)
        ga_ref[0, :rows, :] = jax.nn.sigmoid(_dot(h, w_ref[:, off:off + d])).astype(BF16)
        off += d
        gc_ref[0, :rows, :] = jax.nn.sigmoid(_dot(h, w_ref[:, off:off + d])).astype(BF16)
        for c, (t, ss) in enumerate(zip(chunks, sumsq)):
            is_q = c < qw // LANES
            t = norm_rope(t, ss, (gq_ref if is_q else gk_ref)[...], cos, sin)
            if is_q:
                q_ref[0, :rows, c * LANES:(c + 1) * LANES] = (t * Q_SCALE).astype(BF16)
            else:
                ck = c - qw // LANES
                k_ref[0, :rows, ck * LANES:(ck + 1) * LANES] = t.astype(BF16)

    @pl.when(j < nreal)
    def _():
        body(x_ref[0], cos_ref[...], sin_ref[...], TM)
        for src, dst in zip(cast_in, cast_out):
            dst[...] = src[...].astype(BF16)

    @pl.when(j == nreal)
    def _():
        body(meta_ref[...], cos_ref[:BLK, :], sin_ref[:BLK, :], BLK)


def _cast_rows(arr, steps):
    rows, cols = arr.shape
    r = rows // steps
    ok = rows % steps == 0 and r % HALO == 0 and r * cols * 4 <= CAST_CHUNK_BYTES
    return r if ok else 0


def _stage_a(x, meta_blk, cos2, sin2, gmix, gq2, gk2, bd, w_in_b, cast_srcs=()):
    b, s, d = x.shape
    li = s + BLK
    nreal = s // TM
    qw, kvw, cw = N_HEADS * HEAD_DIM, N_KV_HEADS * HEAD_DIM, d
    const = lambda bi, j: (0, 0)
    rowblk = lambda bi, j: (bi, j, 0)
    out_w = (qw, kvw, kvw, cw, cw, d, d)
    chunk = lambda bi, j: (bi * nreal + jnp.minimum(j, nreal - 1), 0)
    cast_specs = [pl.BlockSpec((_cast_rows(a, b * nreal), a.shape[1]), chunk) for a in cast_srcs]
    outs = pl.pallas_call(
        functools.partial(_proj_kernel, nreal=nreal, d=d, qw=qw, kvw=kvw, cw=cw, n_cast=len(cast_srcs)),
        grid=(b, nreal + 1),
        in_specs=[
            pl.BlockSpec((1, TM, d), lambda bi, j: (bi, jnp.minimum(j, nreal - 1), 0)),
            pl.BlockSpec((BLK, d), const),
            pl.BlockSpec((TM, LANES), lambda bi, j: (j, 0)),
            pl.BlockSpec((TM, LANES), lambda bi, j: (j, 0)),
            pl.BlockSpec((1, d), const),
            pl.BlockSpec((1, LANES), const),
            pl.BlockSpec((1, LANES), const),
            pl.BlockSpec((LANES, LANES), const),
            pl.BlockSpec(w_in_b.shape, const, pipeline_mode=pl.Buffered(1)),
        ] + cast_specs,
        out_specs=[pl.BlockSpec((1, TM, w), rowblk) for w in out_w] + cast_specs,
        out_shape=[jax.ShapeDtypeStruct((b, li, w), BF16) for w in out_w]
        + [jax.ShapeDtypeStruct(a.shape, BF16) for a in cast_srcs],
        compiler_params=_cparams(2),
        name="proj_stage",
    )(x, meta_blk, cos2, sin2, gmix, gq2, gk2, bd, w_in_b, *cast_srcs)
    return outs[:7], outs[7:]


def _attn_kernel(*refs, nblk):
    nq, nkv = ATTN_BLOCKS, ATTN_BLOCKS + 2
    q_refs, k_refs, v_refs = refs[:nq], refs[nq:nq + nkv], refs[nq + nkv:nq + 2 * nkv]
    sink_ref, o_ref = refs[nq + 2 * nkv:]
    kj = lax.broadcasted_iota(jnp.int32, (3 * BLK, BLK), 0)
    qi = lax.broadcasted_iota(jnp.int32, (3 * BLK, BLK), 1)
    band = (kj >= qi) & (kj <= qi + 2 * BLK)
    seg = jnp.right_shift(kj, BLK.bit_length() - 1)
    row = jnp.bitwise_and(kj, BLK - 1)
    lane = lax.broadcasted_iota(jnp.int32, (1, LANES), 1)
    first = lane < HEAD_DIM
    first_rows = lax.broadcasted_iota(jnp.int32, (LANES, 1), 0) < HEAD_DIM
    zero = jnp.zeros((), BF16)
    groups = N_HEADS // N_KV_HEADS
    n_pairs = N_KV_HEADS * HEAD_DIM // LANES
    ok2, kcat, v_t = [], [], []
    for u in range(nq):
        sb = pl.program_id(1) * nq + u
        lo_prev = jnp.where(sb == 0, META_ROW0, jnp.where(sb == nblk - 1, BLK, 0))
        lo_cur = jnp.where(sb == nblk - 1, META_ROW0, 0)
        lo_next = jnp.where(sb == nblk - 2, BLK, 0)
        lo = jnp.where(seg == 0, lo_prev, jnp.where(seg == 1, lo_cur, lo_next))
        ok = band & (row >= lo)
        ok2.append(jnp.concatenate([ok, ok], axis=1))
        for m in range(n_pairs):
            cols = slice(m * LANES, (m + 1) * LANES)
            kcat.append(jnp.concatenate([r[0, :, cols] for r in k_refs[u:u + 3]], axis=0))
            vcat = jnp.concatenate([r[0, :, cols] for r in v_refs[u:u + 3]], axis=0)
            v_t.append(jnp.concatenate([jnp.transpose(vcat.astype(F32)).astype(BF16),
                                        jnp.ones((DEN_ROWS, 3 * BLK), BF16)], axis=0))

    per_blk = n_pairs * groups

    def scores(n):
        u, c = divmod(n, per_blk)
        qc = q_refs[u][0, :, c * LANES:(c + 1) * LANES]
        lhs = jnp.concatenate([jnp.where(first, qc, zero), jnp.where(first, zero, qc)], axis=0)
        return _dot_nt(kcat[u * n_pairs + c // groups], lhs)

    n_chunks = nq * per_blk
    pending = [scores(n) for n in range(ATTN_LOOKAHEAD)]
    for n in range(n_chunks):
        u, c = divmod(n, per_blk)
        m, i = divmod(c, groups)
        if n + ATTN_LOOKAHEAD < n_chunks:
            pending.append(scores(n + ATTN_LOOKAHEAD))
        s_t = jnp.where(ok2[u], pending.pop(0), NEG)
        sink = sink_ref[m, :, 2 * i * BLK:(2 * i + 2) * BLK]
        mx = jnp.maximum(jnp.max(s_t, axis=0, keepdims=True), sink)
        p = jnp.exp2(s_t - mx)
        o_t = _dot(v_t[u * n_pairs + m], p.astype(BF16))
        den = o_t[LANES:LANES + 1, :] + jnp.exp2(sink - mx)
        o_t = o_t[:LANES, :] / den
        blk = jnp.where(first_rows, o_t[:, :BLK], o_t[:, BLK:])
        o_ref[0, u * BLK:(u + 1) * BLK, c * LANES:(c + 1) * LANES] = jnp.transpose(blk).astype(BF16)


def _stage_b(q, k, v, sink_rows):
    b, li, qw = q.shape
    kvw = k.shape[-1]
    nblk = li // BLK
    nq = ATTN_BLOCKS
    q_specs = [pl.BlockSpec((1, BLK, qw), lambda bi, j, u=u: (bi, jnp.minimum(j * nq + u, nblk - 1), 0))
               for u in range(nq)]
    kv_specs = [pl.BlockSpec((1, BLK, kvw), lambda bi, j, u=u: (bi, (j * nq + u + nblk - 1) % nblk, 0))
                for u in range(nq + 2)]
    return pl.pallas_call(
        functools.partial(_attn_kernel, nblk=nblk),
        grid=(b, -(-nblk // nq)),
        in_specs=q_specs + kv_specs + kv_specs
        + [pl.BlockSpec(sink_rows.shape, lambda bi, j: (0, 0, 0))],
        out_specs=pl.BlockSpec((1, nq * BLK, qw), lambda bi, j: (bi, j, 0)),
        out_shape=jax.ShapeDtypeStruct((b, li, qw), BF16),
        compiler_params=_cparams(2),
        name="attn_stage",
    )(*([q] * nq + [k] * (nq + 2) + [v] * (nq + 2) + [sink_rows]))


def _mix_kernel(x_ref, meta_ref, attn_ref, cb_ref, s_ref, sp_ref, sn_ref, ga_ref, gc_ref,
                wconv_ref, gffn_ref, wa_ref, wc_ref, wo_ref, wr_ref,
                x1_ref, h2_ref, aff_ref, *, nreal):
    j = pl.program_id(1)

    def body(x, rows, n_pad):
        attn_p = _dot(attn_ref[0, :rows, :], wa_ref[...])
        s = s_ref[0, :rows, :].astype(F32)
        ridx = lax.broadcasted_iota(jnp.int32, (rows, 1), 0)
        s_prev = jnp.where(ridx == 0, sp_ref[0, HALO - 1:HALO, :].astype(F32), pltpu.roll(s, 1, 0))
        s_next = jnp.where(ridx == rows - 1, sn_ref[0, 0:1, :].astype(F32), pltpu.roll(s, rows - 1, 0))
        conv = wconv_ref[0:1, :] * s_prev + wconv_ref[1:2, :] * s + wconv_ref[2:3, :] * s_next
        conv_in = (cb_ref[0, :rows, :].astype(F32) * conv).astype(BF16)
        conv_p = _dot(conv_in, wc_ref[...])
        merged = ga_ref[0, :rows, :].astype(F32) * attn_p + gc_ref[0, :rows, :].astype(F32) * conv_p
        x1 = x + _dot(merged.astype(BF16), wo_ref[...])
        x1_ref[0, :rows, :] = x1
        ms = jnp.mean(x1 * x1, axis=-1, keepdims=True)
        h2 = (x1 * lax.rsqrt(ms + EPS) * gffn_ref[...]).astype(BF16)
        h2_ref[0, :rows, :] = h2
        logits = _dot_nt(wr_ref[...], h2)
        e = jnp.exp(logits - jnp.max(logits, axis=0, keepdims=True))
        aff = e / jnp.sum(e, axis=0, keepdims=True)
        if n_pad:
            aff = jnp.where(lax.broadcasted_iota(jnp.int32, aff.shape, 1) < n_pad, -1.0, aff)
        aff_ref[0, :, :rows] = aff

    @pl.when(j < nreal)
    def _():
        body(x_ref[0], TM, 0)

    @pl.when(j == nreal)
    def _():
        body(meta_ref[...], BLK, META_ROW0)


def _stage_c(x, meta_blk, attn, cb, sprod, ga, gc, wconv, gffn, wa, wc, wo, wr_t):
    b, s, d = x.shape
    li = s + BLK
    nreal = s // TM
    nhalo = li // HALO
    const = lambda bi, j: (0, 0)
    rowblk = lambda bi, j: (bi, j, 0)
    halo_prev = lambda bi, j: (bi, (j * (TM // HALO) + nhalo - 1) % nhalo, 0)
    halo_next = lambda bi, j: (bi, jnp.where(j == nreal, 0, (j + 1) * (TM // HALO)), 0)
    n_exp = wr_t.shape[0]
    wspec = lambda w: pl.BlockSpec(w.shape, const, pipeline_mode=pl.Buffered(1))
    return pl.pallas_call(
        functools.partial(_mix_kernel, nreal=nreal),
        grid=(b, nreal + 1),
        in_specs=[
            pl.BlockSpec((1, TM, d), lambda bi, j: (bi, jnp.minimum(j, nreal - 1), 0)),
            pl.BlockSpec((BLK, d), const),
            pl.BlockSpec((1, TM, d), rowblk),
            pl.BlockSpec((1, TM, d), rowblk),
            pl.BlockSpec((1, TM, d), rowblk),
            pl.BlockSpec((1, HALO, d), halo_prev),
            pl.BlockSpec((1, HALO, d), halo_next),
            pl.BlockSpec((1, TM, d), rowblk),
            pl.BlockSpec((1, TM, d), rowblk),
            pl.BlockSpec(wconv.shape, const),
            pl.BlockSpec((1, d), const),
            wspec(wa), wspec(wc), wspec(wo), wspec(wr_t),
        ],
        out_specs=[
            pl.BlockSpec((1, TM, d), rowblk),
            pl.BlockSpec((1, TM, d), rowblk),
            pl.BlockSpec((1, n_exp, TM), lambda bi, j: (bi, 0, j)),
        ],
        out_shape=[
            jax.ShapeDtypeStruct((b, li, d), F32),
            jax.ShapeDtypeStruct((b, li, d), BF16),
            jax.ShapeDtypeStruct((b, n_exp, li), F32),
        ],
        compiler_params=_cparams(2),
        name="mix_stage",
    )(x, meta_blk, attn, cb, sprod, sprod, sprod, ga, gc, wconv, gffn, wa, wc, wo, wr_t)


def _expert_kernel(xs_ref, wg_ref, wu_ref, wd_ref, y_ref):
    x = xs_ref[0]
    g = _dot(x, wg_ref[0])
    u = _dot(x, wu_ref[0])
    he = (g * jax.nn.sigmoid(g) * u).astype(BF16)
    y_ref[0] = _dot(he, wd_ref[0]).astype(y_ref.dtype)


def _stage_d(xs, wg, wu, wd):
    e, cap_p, d = xs.shape
    de = wg.shape[-1]
    tile = lambda ei, t: (ei, t, 0)
    wmap = lambda ei, t: (ei, 0, 0)
    return pl.pallas_call(
        _expert_kernel,
        grid=(e, cap_p // TE),
        in_specs=[
            pl.BlockSpec((1, TE, d), tile),
            pl.BlockSpec((1, d, de), wmap),
            pl.BlockSpec((1, d, de), wmap),
            pl.BlockSpec((1, de, d), wmap),
        ],
        out_specs=pl.BlockSpec((1, TE, d), tile),
        out_shape=jax.ShapeDtypeStruct((e, cap_p, d), BF16),
        compiler_params=_cparams(2),
        name="expert_stage",
    )(xs, wg, wu, wd)


def _combine_kernel(start_ref, rounds_ref, x1_ref, slot_ref, gate_ref, ye_ref, o_ref, ybuf, sem, *,
                    n_exp, nsteps):
    lin = pl.program_id(0) * pl.num_programs(1) + pl.program_id(1)

    def chunk_start(step, e, rnd):
        return jnp.minimum(start_ref[step * n_exp + e] + rnd * CH, ye_ref.shape[1] - CH)

    def chunk_copy(step, e, rnd, buf):
        st = pl.multiple_of(chunk_start(step, e, rnd), HALO)
        return pltpu.make_async_copy(ye_ref.at[e, pl.ds(st, CH), :],
                                     ybuf.at[buf, pl.ds(e * CH, CH), :], sem.at[buf])

    def fetch(step, rnd, buf):
        for e in range(n_exp):
            chunk_copy(step, e, rnd, buf).start()

    def wait(step, rnd, buf):
        for e in range(n_exp):
            chunk_copy(step, e, rnd, buf).wait()

    buf = lin % 2

    @pl.when(lin == 0)
    def _():
        fetch(0, 0, 0)

    @pl.when(lin + 1 < nsteps)
    def _():
        fetch(lin + 1, 0, 1 - buf)

    slot = jnp.transpose(slot_ref[0])
    gate = jnp.transpose(gate_ref[0])
    lane_c = lax.broadcasted_iota(jnp.int32, (1, CH), 1)

    def gathered(rnd):
        perm = []
        for e in range(n_exp):
            st = chunk_start(lin, e, rnd)
            first_lane = start_ref[lin * n_exp + e] + rnd * CH - st
            lanes = jnp.where(lane_c >= first_lane, lane_c, CH + 1).astype(F32).astype(BF16)
            offset = jnp.clip(slot[:, e:e + 1] - st, -1, CH).astype(F32).astype(BF16)
            perm.append(jnp.where(offset == lanes, gate[:, e:e + 1].astype(BF16), jnp.zeros((), BF16)))
        return _dot(jnp.concatenate(perm, axis=1), ybuf[buf])

    wait(lin, 0, buf)
    o_ref[0] = x1_ref[0] + gathered(0)

    def extra(rnd, carry):
        fetch(lin, rnd, buf)
        wait(lin, rnd, buf)
        o_ref[0] += gathered(rnd)
        return carry

    lax.fori_loop(1, rounds_ref[lin], extra, 0)


def _stage_e(x1, slot, gate, ye, start_al, rounds, s):
    b, li, d = x1.shape
    n_exp = slot.shape[1]
    assert CH + 1 <= BYTE
    ntile = s // TM
    tile = lambda bi, j, *_: (bi, j, 0)
    return pl.pallas_call(
        functools.partial(_combine_kernel, n_exp=n_exp, nsteps=b * ntile),
        grid_spec=pltpu.PrefetchScalarGridSpec(
            num_scalar_prefetch=2,
            grid=(b, ntile),
            in_specs=[
                pl.BlockSpec((1, TM, d), tile),
                pl.BlockSpec((1, n_exp, TM), lambda bi, j, *_: (bi, 0, j)),
                pl.BlockSpec((1, n_exp, TM), lambda bi, j, *_: (bi, 0, j)),
                pl.BlockSpec(memory_space=pl.ANY),
            ],
            out_specs=pl.BlockSpec((1, TM, d), tile),
            scratch_shapes=[pltpu.VMEM((2, n_exp * CH, d), BF16), pltpu.SemaphoreType.DMA((2,))],
        ),
        out_shape=jax.ShapeDtypeStruct((b, s, d), F32),
        compiler_params=_cparams(2),
        name="combine_stage",
    )(start_al, rounds, x1, slot, gate, ye)


def _q_head_perm():
    groups = N_HEADS // N_KV_HEADS
    cols = []
    for m in range(N_KV_HEADS // 2):
        for i in range(groups):
            for h in (2 * groups * m + i, 2 * groups * m + groups + i):
                cols.extend(range(h * HEAD_DIM, (h + 1) * HEAD_DIM))
    return np.asarray(cols, np.int32)


def _rope_tables(s):
    pos = jnp.concatenate([N_META + jnp.arange(s, dtype=F32), jnp.zeros((META_ROW0,), F32),
                           jnp.arange(N_META, dtype=F32)])
    inv = ROPE_THETA ** (-jnp.arange(0, HEAD_DIM, 2, dtype=F32) / HEAD_DIM)
    ang = pos[:, None] * inv[None, :]
    reps = LANES // (HEAD_DIM // 2)
    cos2 = jnp.tile(jnp.cos(ang), (1, reps))
    sgn = jnp.where((jnp.arange(LANES) % HEAD_DIM) < HEAD_DIM // 2, -1.0, 1.0).astype(F32)
    sin2 = jnp.tile(jnp.sin(ang), (1, reps)) * sgn[None, :]
    return cos2, sin2


def _sink_rows(sink):
    groups = N_HEADS // N_KV_HEADS
    heads = np.asarray([[2 * groups * m + groups * half + i
                         for i in range(groups) for half in range(2)]
                        for m in range(N_KV_HEADS // 2)], np.int32)
    rows = jnp.repeat(sink.astype(F32)[heads] * LOG2E, BLK, axis=1)
    return rows[:, None, :]


def _threshold_kernel(aff_ref, thr_ref, *, cap):
    bits = pltpu.bitcast(aff_ref[...], jnp.int32)

    def step(i, prefix):
        cand = prefix | jnp.left_shift(jnp.int32(1), 30 - i)
        count = jnp.sum((bits >= cand[None]).astype(jnp.int32), axis=2, keepdims=True)
        return jnp.where(jnp.sum(count, axis=0) >= cap, cand, prefix)

    prefix = lax.fori_loop(0, 31, step, jnp.zeros((bits.shape[1], 1), jnp.int32))
    thr_ref[...] = jnp.broadcast_to(pltpu.bitcast(prefix, F32), thr_ref.shape)


def _thresholds(aff_t, cap):
    n_exp = aff_t.shape[1]
    out = pl.pallas_call(
        functools.partial(_threshold_kernel, cap=cap),
        out_shape=jax.ShapeDtypeStruct((n_exp, LANES), F32),
        compiler_params=pltpu.CompilerParams(vmem_limit_bytes=VMEM_LIMIT),
        name="threshold_stage",
    )(aff_t)
    return out[:, 0]


def _prefix_counts(mask):
    b, e, li = mask.shape
    g = li // LANES
    tri_l = jnp.asarray(np.tri(LANES, dtype=np.float32).T, BF16)
    tri_g = jnp.asarray(np.tri(b * g, k=-1, dtype=np.float32).T, BF16)
    local = jnp.einsum("begl,lm->begm", mask.reshape(b, e, g, LANES).astype(BF16), tri_l,
                       preferred_element_type=F32)
    totals = jnp.transpose(local[:, :, :, LANES - 1], (1, 0, 2)).reshape(e, b * g)
    before = jnp.einsum("eg,gh->eh", totals.astype(BF16), tri_g, preferred_element_type=F32)
    before = jnp.transpose(before.reshape(e, b, g), (1, 0, 2))
    return local.astype(jnp.int32), before.astype(jnp.int32)


def _route(aff, s):
    b, n_exp, li = aff.shape
    grp_seq = li // LANES
    ngrp = b * grp_seq
    n_tok = b * (s + N_META)
    cap = max(1, CAPACITY_FACTOR * n_tok // n_exp)
    cap_p = max(-(-cap // TE) * TE, CH)
    thr = _thresholds(aff, cap)[None, :, None]
    above = aff > thr
    equal = aff == thr
    room = cap - jnp.sum(above, axis=(0, 2), dtype=jnp.int32)[None, :, None]

    def earliest_ties(_):
        local, before = _prefix_counts(equal)
        eq_upto = (local + before[:, :, :, None]).reshape(b, n_exp, li)
        eq_real_end = eq_upto[:, :, s - 1:s]
        eq_before_seq = eq_upto[:, :, 0:1] - equal[:, :, 0:1]
        eq_real = eq_real_end - eq_before_seq
        eq_meta = eq_upto[:, :, li - 1:li] - eq_real_end
        is_real = (jnp.arange(li) < s)[None, None, :]
        rank = eq_upto + jnp.where(is_real, eq_meta, -eq_real)
        return above | (equal & (rank <= room))

    all_ties_fit = jnp.all(jnp.sum(equal, axis=(0, 2), dtype=jnp.int32)[None, :, None] == room)
    sel = lax.cond(all_ties_fit, lambda _: above | equal, earliest_ties, None)

    local, grp_before = _prefix_counts(sel)
    upto = (local + grp_before[:, :, :, None]).reshape(b, n_exp, li)
    slot = jnp.where(sel, upto - 1, UNSELECTED)
    gate = jnp.where(sel, aff, 0.0)

    tile_grp = TM // LANES
    start = grp_before[:, :, 0:s // LANES:tile_grp]
    end = grp_before[:, :, tile_grp:s // LANES + 1:tile_grp]
    start_al = start // HALO * HALO
    rounds = jnp.maximum(jnp.max(-(-(end - start_al) // CH), axis=1), 1)
    start_al = jnp.transpose(start_al, (0, 2, 1)).reshape(-1)

    local = jnp.transpose(local, (1, 0, 2, 3)).reshape(n_exp, ngrp, LANES)
    grp_before = jnp.transpose(grp_before, (1, 0, 2)).reshape(n_exp, ngrp)
    grp_end = jnp.concatenate([grp_before[:, 1:], jnp.full((n_exp, 1), cap, jnp.int32)], axis=1)
    gidx = jnp.arange(ngrp, dtype=jnp.int32)[None, :]
    byte = lambda v: jnp.stack([v // BYTE, v % BYTE], axis=-1)
    table = jnp.concatenate([local, byte(grp_before), jnp.broadcast_to(byte(gidx), (n_exp, ngrp, 2))],
                            axis=-1).astype(BF16)
    slots = jnp.arange(cap_p, dtype=jnp.int32)[None, :, None]
    in_grp = ((grp_before[:, None, :] <= slots) & (slots < grp_end[:, None, :])).astype(BF16)
    hit = jnp.einsum("esg,egc->esc", in_grp, table, preferred_element_type=F32).astype(jnp.int32)
    j = slots[:, :, 0] - (hit[:, :, LANES] * BYTE + hit[:, :, LANES + 1])
    lane = jnp.sum(hit[:, :, :LANES] <= j[:, :, None], axis=-1, dtype=jnp.int32)
    rows = (hit[:, :, LANES + 2] * BYTE + hit[:, :, LANES + 3]) * LANES + lane
    rows = jnp.where(slots[:, :, 0] < cap, rows, 0)
    return rows, slot, gate, start_al, rounds.reshape(-1)


def _project(x, meta_blk, prm, cast_srcs=()):
    b, s, d = x.shape
    assert s % TM == 0 and d % LANES == 0
    cos2, sin2 = _rope_tables(s)
    return _stage_a(x, meta_blk, cos2, sin2, prm["gmix"], prm["gq2"], prm["gk2"], prm["bd"],
                    prm["w_in"], cast_srcs)


def _encoder(x, meta_blk, prm, projected, expert_w):
    b, s, d = x.shape
    li = s + BLK
    q, k, v, cb, sprod, ga, gc = projected
    attn = _stage_b(q, k, v, prm["sink_rows"])
    x1, h2, aff_t = _stage_c(x, meta_blk, attn, cb, sprod, ga, gc, prm["wconv"], prm["gffn"],
                             prm["wa"], prm["wc"], prm["wo"], prm["wr_t"])
    rows, slot, gate, start_al, rounds = _route(aff_t, s)
    xs = h2.reshape(b * li, d)[rows]
    ye = _stage_d(xs, *expert_w)
    return _stage_e(x1, slot, gate, ye, start_al, rounds, s)


def kernel(x_prompt, x_sample, meta_tokens, g_mix, w_in, g_q, g_k, sink_logits, w_conv, w_attn_out,
           w_conv_out, w_out, g_ffn, w_router, w_expert_gate, w_expert_up, w_expert_down):
    assert w_in.shape[0] == 1, "single-layer block"
    d = x_prompt.shape[-1]
    qw = N_HEADS * HEAD_DIM
    perm = _q_head_perm()
    w_in0 = w_in[0]
    lane = np.arange(LANES)
    prm = {
        "w_in": jnp.concatenate([w_in0[:, :qw][:, perm], w_in0[:, qw:]], axis=1).astype(BF16),
        "gmix": g_mix[0][None, :].astype(F32),
        "gq2": jnp.tile(g_q[0], LANES // HEAD_DIM)[None, :].astype(F32),
        "gk2": jnp.tile(g_k[0], LANES // HEAD_DIM)[None, :].astype(F32),
        "bd": jnp.asarray((lane[:, None] // HEAD_DIM) == (lane[None, :] // HEAD_DIM), BF16),
        "sink_rows": _sink_rows(sink_logits[0]),
        "wconv": w_conv[0].astype(F32),
        "gffn": g_ffn[0][None, :].astype(F32),
        "wa": w_attn_out[0][perm, :].astype(BF16),
        "wc": w_conv_out[0].astype(BF16),
        "wo": w_out[0].astype(BF16),
        "wr_t": jnp.transpose(w_router[0]).astype(BF16),
    }
    meta_blk = jnp.concatenate([jnp.zeros((META_ROW0, d), F32), meta_tokens.astype(F32)], axis=0)
    expert_w = [w_expert_gate[0], w_expert_up[0], w_expert_down[0]]
    flat = [w.astype(F32).reshape(-1, w.shape[-1]) for w in expert_w]
    steps = x_prompt.shape[0] * (x_prompt.shape[1] // TM)
    if all(_cast_rows(w, steps) for w in flat):
        proj_prompt, cast = _project(x_prompt, meta_blk, prm, flat)
        expert_w = [c.reshape(w.shape) for c, w in zip(cast, expert_w)]
    else:
        proj_prompt, _ = _project(x_prompt, meta_blk, prm)
        expert_w = [w.astype(BF16) for w in expert_w]
    proj_sample, _ = _project(x_sample, meta_blk, prm)
    return (_encoder(x_prompt, meta_blk, prm, proj_prompt, expert_w),
            _encoder(x_sample, meta_blk, prm, proj_sample, expert_w))
```

```python
import functools

import numpy as np
import jax
import jax.numpy as jnp
from jax import lax
from jax.experimental import pallas as pl
from jax.experimental.pallas import tpu as pltpu

N_META = 16
N_HEADS = 16
N_KV_HEADS = 4
HEAD_DIM = 64
CAPACITY_FACTOR = 2
ROPE_THETA = 10000.0
EPS = 1e-6
NEG = -1e30

LANES = 128
BLK = 128
META_ROW0 = BLK - N_META
TM = 512
TE = 1056
HALO = 16
DEN_ROWS = 16
ATTN_LOOKAHEAD = 4
ATTN_BLOCKS = 3
CH = 128
UNSELECTED = -(1 << 30)
BYTE = 256
LOG2E = 1.4426950408889634
Q_SCALE = HEAD_DIM ** -0.5 * LOG2E
VMEM_LIMIT = 56 * 1024 * 1024
CAST_CHUNK_BYTES = 2 * 1024 * 1024

F32 = jnp.float32
BF16 = jnp.bfloat16


def _dot(a, b):
    return jnp.dot(a, b, preferred_element_type=F32)


def _dot_nt(a, b):
    return lax.dot_general(a, b, (((1,), (1,)), ((), ())), preferred_element_type=F32)


def _cparams(n_axes):
    return pltpu.CompilerParams(dimension_semantics=("arbitrary",) * n_axes,
                                vmem_limit_bytes=VMEM_LIMIT)


def _proj_kernel(x_ref, meta_ref, cos_ref, sin_ref, gmix_ref, gq_ref, gk_ref, bd_ref, w_ref, *refs,
                 nreal, d, qw, kvw, cw, n_cast):
    cast_in, refs = refs[:n_cast], refs[n_cast:]
    (q_ref, k_ref, v_ref, cb_ref, s_ref, ga_ref, gc_ref), cast_out = refs[:7], refs[7:]
    j = pl.program_id(1)
    lane = lax.broadcasted_iota(jnp.int32, (1, LANES), 1)
    first_half = jnp.bitwise_and(lane, HEAD_DIM - 1) < (HEAD_DIM // 2)

    def norm_rope(t, ss, g, cos, sin):
        t = t * lax.rsqrt(ss * (1.0 / HEAD_DIM) + EPS) * g
        partner = jnp.where(first_half, pltpu.roll(t, LANES - HEAD_DIM // 2, 1),
                            pltpu.roll(t, HEAD_DIM // 2, 1))
        return t * cos + partner * sin

    def body(x, cos, sin, rows):
        ms = jnp.mean(x * x, axis=-1, keepdims=True)
        h = (x * lax.rsqrt(ms + EPS) * gmix_ref[...]).astype(BF16)
        qk = _dot(h, w_ref[:, :qw + kvw])
        off = qw + kvw
        v_ref[0, :rows, :] = _dot(h, w_ref[:, off:off + kvw]).astype(BF16)
        off += kvw
        chunks = [qk[:, c * LANES:(c + 1) * LANES] for c in range((qw + kvw) // LANES)]
        sumsq = [_dot((t * t).astype(BF16), bd_ref[...]) for t in chunks]
        cb_ref[0, :rows, :] = _dot(h, w_ref[:, off:off + cw]).astype(BF16)
        off += cw
        cc = _dot(h, w_ref[:, off:off + cw])
        off += cw
        cu = _dot(h, w_ref[:, off:off + cw])
        off += cw
        s_ref[0, :rows, :] = (cc * cu).astype(BF16)
        ga_ref[0, :rows, :] = jax.nn.sigmoid(_dot(h, w_ref[:, off:off + d])).astype(BF16)
        off += d
        gc_ref[0, :rows, :] = jax.nn.sigmoid(_dot(h, w_ref[:, off:off + d])).astype(BF16)
        for c, (t, ss) in enumerate(zip(chunks, sumsq)):
            is_q = c < qw // LANES
            t = norm_rope(t, ss, (gq_ref if is_q else gk_ref)[...], cos, sin)
            if is_q:
                q_ref[0, :rows, c * LANES:(c + 1) * LANES] = (t * Q_SCALE).astype(BF16)
            else:
                ck = c - qw // LANES
                k_ref[0, :rows, ck * LANES:(ck + 1) * LANES] = t.astype(BF16)

    @pl.when(j < nreal)
    def _():
        body(x_ref[0], cos_ref[...], sin_ref[...], TM)
        for src, dst in zip(cast_in, cast_out):
            dst[...] = src[...].astype(BF16)

    @pl.when(j == nreal)
    def _():
        body(meta_ref[...], cos_ref[:BLK, :], sin_ref[:BLK, :], BLK)


def _cast_rows(arr, steps):
    rows, cols = arr.shape
    r = rows // steps
    ok = rows % steps == 0 and r % HALO == 0 and r * cols * 4 <= CAST_CHUNK_BYTES
    return r if ok else 0


def _stage_a(x, meta_blk, cos2, sin2, gmix, gq2, gk2, bd, w_in_b, cast_srcs=()):
    b, s, d = x.shape
    li = s + BLK
    nreal = s // TM
    qw, kvw, cw = N_HEADS * HEAD_DIM, N_KV_HEADS * HEAD_DIM, d
    const = lambda bi, j: (0, 0)
    rowblk = lambda bi, j: (bi, j, 0)
    out_w = (qw, kvw, kvw, cw, cw, d, d)
    chunk = lambda bi, j: (bi * nreal + jnp.minimum(j, nreal - 1), 0)
    cast_specs = [pl.BlockSpec((_cast_rows(a, b * nreal), a.shape[1]), chunk) for a in cast_srcs]
    outs = pl.pallas_call(
        functools.partial(_proj_kernel, nreal=nreal, d=d, qw=qw, kvw=kvw, cw=cw, n_cast=len(cast_srcs)),
        grid=(b, nreal + 1),
        in_specs=[
            pl.BlockSpec((1, TM, d), lambda bi, j: (bi, jnp.minimum(j, nreal - 1), 0)),
            pl.BlockSpec((BLK, d), const),
            pl.BlockSpec((TM, LANES), lambda bi, j: (j, 0)),
            pl.BlockSpec((TM, LANES), lambda bi, j: (j, 0)),
            pl.BlockSpec((1, d), const),
            pl.BlockSpec((1, LANES), const),
            pl.BlockSpec((1, LANES), const),
            pl.BlockSpec((LANES, LANES), const),
            pl.BlockSpec(w_in_b.shape, const, pipeline_mode=pl.Buffered(1)),
        ] + cast_specs,
        out_specs=[pl.BlockSpec((1, TM, w), rowblk) for w in out_w] + cast_specs,
        out_shape=[jax.ShapeDtypeStruct((b, li, w), BF16) for w in out_w]
        + [jax.ShapeDtypeStruct(a.shape, BF16) for a in cast_srcs],
        compiler_params=_cparams(2),
        name="proj_stage",
    )(x, meta_blk, cos2, sin2, gmix, gq2, gk2, bd, w_in_b, *cast_srcs)
    return outs[:7], outs[7:]


def _attn_kernel(*refs, nblk):
    nq, nkv = ATTN_BLOCKS, ATTN_BLOCKS + 2
    q_refs, k_refs, v_refs = refs[:nq], refs[nq:nq + nkv], refs[nq + nkv:nq + 2 * nkv]
    sink_ref, o_ref = refs[nq + 2 * nkv:]
    kj = lax.broadcasted_iota(jnp.int32, (3 * BLK, BLK), 0)
    qi = lax.broadcasted_iota(jnp.int32, (3 * BLK, BLK), 1)
    band = (kj >= qi) & (kj <= qi + 2 * BLK)
    seg = jnp.right_shift(kj, BLK.bit_length() - 1)
    row = jnp.bitwise_and(kj, BLK - 1)
    lane = lax.broadcasted_iota(jnp.int32, (1, LANES), 1)
    first = lane < HEAD_DIM
    first_rows = lax.broadcasted_iota(jnp.int32, (LANES, 1), 0) < HEAD_DIM
    zero = jnp.zeros((), BF16)
    groups = N_HEADS // N_KV_HEADS
    n_pairs = N_KV_HEADS * HEAD_DIM // LANES
    ok2, kcat, v_t = [], [], []
    for u in range(nq):
        sb = pl.program_id(1) * nq + u
        lo_prev = jnp.where(sb == 0, META_ROW0, jnp.where(sb == nblk - 1, BLK, 0))
        lo_cur = jnp.where(sb == nblk - 1, META_ROW0, 0)
        lo_next = jnp.where(sb == nblk - 2, BLK, 0)
        lo = jnp.where(seg == 0, lo_prev, jnp.where(seg == 1, lo_cur, lo_next))
        ok = band & (row >= lo)
        ok2.append(jnp.concatenate([ok, ok], axis=1))
        for m in range(n_pairs):
            cols = slice(m * LANES, (m + 1) * LANES)
            kcat.append(jnp.concatenate([r[0, :, cols] for r in k_refs[u:u + 3]], axis=0))
            vcat = jnp.concatenate([r[0, :, cols] for r in v_refs[u:u + 3]], axis=0)
            v_t.append(jnp.concatenate([jnp.transpose(vcat.astype(F32)).astype(BF16),
                                        jnp.ones((DEN_ROWS, 3 * BLK), BF16)], axis=0))

    per_blk = n_pairs * groups

    def scores(n):
        u, c = divmod(n, per_blk)
        qc = q_refs[u][0, :, c * LANES:(c + 1) * LANES]
        lhs = jnp.concatenate([jnp.where(first, qc, zero), jnp.where(first, zero, qc)], axis=0)
        return _dot_nt(kcat[u * n_pairs + c // groups], lhs)

    n_chunks = nq * per_blk
    pending = [scores(n) for n in range(ATTN_LOOKAHEAD)]
    for n in range(n_chunks):
        u, c = divmod(n, per_blk)
        m, i = divmod(c, groups)
        if n + ATTN_LOOKAHEAD < n_chunks:
            pending.append(scores(n + ATTN_LOOKAHEAD))
        s_t = jnp.where(ok2[u], pending.pop(0), NEG)
        sink = sink_ref[m, :, 2 * i * BLK:(2 * i + 2) * BLK]
        mx = jnp.maximum(jnp.max(s_t, axis=0, keepdims=True), sink)
        p = jnp.exp2(s_t - mx)
        o_t = _dot(v_t[u * n_pairs + m], p.astype(BF16))
        den = o_t[LANES:LANES + 1, :] + jnp.exp2(sink - mx)
        o_t = o_t[:LANES, :] / den
        blk = jnp.where(first_rows, o_t[:, :BLK], o_t[:, BLK:])
        o_ref[0, u * BLK:(u + 1) * BLK, c * LANES:(c + 1) * LANES] = jnp.transpose(blk).astype(BF16)


def _stage_b(q, k, v, sink_rows):
    b, li, qw = q.shape
    kvw = k.shape[-1]
    nblk = li // BLK
    nq = ATTN_BLOCKS
    q_specs = [pl.BlockSpec((1, BLK, qw), lambda bi, j, u=u: (bi, jnp.minimum(j * nq + u, nblk - 1), 0))
               for u in range(nq)]
    kv_specs = [pl.BlockSpec((1, BLK, kvw), lambda bi, j, u=u: (bi, (j * nq + u + nblk - 1) % nblk, 0))
                for u in range(nq + 2)]
    return pl.pallas_call(
        functools.partial(_attn_kernel, nblk=nblk),
        grid=(b, -(-nblk // nq)),
        in_specs=q_specs + kv_specs + kv_specs
        + [pl.BlockSpec(sink_rows.shape, lambda bi, j: (0, 0, 0))],
        out_specs=pl.BlockSpec((1, nq * BLK, qw), lambda bi, j: (bi, j, 0)),
        out_shape=jax.ShapeDtypeStruct((b, li, qw), BF16),
        compiler_params=_cparams(2),
        name="attn_stage",
    )(*([q] * nq + [k] * (nq + 2) + [v] * (nq + 2) + [sink_rows]))


def _mix_kernel(x_ref, meta_ref, attn_ref, cb_ref, s_ref, sp_ref, sn_ref, ga_ref, gc_ref,
                wconv_ref, gffn_ref, wa_ref, wc_ref, wo_ref, wr_ref,
                x1_ref, h2_ref, aff_ref, *, nreal):
    j = pl.program_id(1)

    def body(x, rows, n_pad):
        attn_p = _dot(attn_ref[0, :rows, :], wa_ref[...])
        s = s_ref[0, :rows, :].astype(F32)
        ridx = lax.broadcasted_iota(jnp.int32, (rows, 1), 0)
        s_prev = jnp.where(ridx == 0, sp_ref[0, HALO - 1:HALO, :].astype(F32), pltpu.roll(s, 1, 0))
        s_next = jnp.where(ridx == rows - 1, sn_ref[0, 0:1, :].astype(F32), pltpu.roll(s, rows - 1, 0))
        conv = wconv_ref[0:1, :] * s_prev + wconv_ref[1:2, :] * s + wconv_ref[2:3, :] * s_next
        conv_in = (cb_ref[0, :rows, :].astype(F32) * conv).astype(BF16)
        conv_p = _dot(conv_in, wc_ref[...])
        merged = ga_ref[0, :rows, :].astype(F32) * attn_p + gc_ref[0, :rows, :].astype(F32) * conv_p
        x1 = x + _dot(merged.astype(BF16), wo_ref[...])
        x1_ref[0, :rows, :] = x1
        ms = jnp.mean(x1 * x1, axis=-1, keepdims=True)
        h2 = (x1 * lax.rsqrt(ms + EPS) * gffn_ref[...]).astype(BF16)
        h2_ref[0, :rows, :] = h2
        logits = _dot_nt(wr_ref[...], h2)
        e = jnp.exp(logits - jnp.max(logits, axis=0, keepdims=True))
        aff = e / jnp.sum(e, axis=0, keepdims=True)
        if n_pad:
            aff = jnp.where(lax.broadcasted_iota(jnp.int32, aff.shape, 1) < n_pad, -1.0, aff)
        aff_ref[0, :, :rows] = aff

    @pl.when(j < nreal)
    def _():
        body(x_ref[0], TM, 0)

    @pl.when(j == nreal)
    def _():
        body(meta_ref[...], BLK, META_ROW0)


def _stage_c(x, meta_blk, attn, cb, sprod, ga, gc, wconv, gffn, wa, wc, wo, wr_t):
    b, s, d = x.shape
    li = s + BLK
    nreal = s // TM
    nhalo = li // HALO
    const = lambda bi, j: (0, 0)
    rowblk = lambda bi, j: (bi, j, 0)
    halo_prev = lambda bi, j: (bi, (j * (TM // HALO) + nhalo - 1) % nhalo, 0)
    halo_next = lambda bi, j: (bi, jnp.where(j == nreal, 0, (j + 1) * (TM // HALO)), 0)
    n_exp = wr_t.shape[0]
    wspec = lambda w: pl.BlockSpec(w.shape, const, pipeline_mode=pl.Buffered(1))
    return pl.pallas_call(
        functools.partial(_mix_kernel, nreal=nreal),
        grid=(b, nreal + 1),
        in_specs=[
            pl.BlockSpec((1, TM, d), lambda bi, j: (bi, jnp.minimum(j, nreal - 1), 0)),
            pl.BlockSpec((BLK, d), const),
            pl.BlockSpec((1, TM, d), rowblk),
            pl.BlockSpec((1, TM, d), rowblk),
            pl.BlockSpec((1, TM, d), rowblk),
            pl.BlockSpec((1, HALO, d), halo_prev),
            pl.BlockSpec((1, HALO, d), halo_next),
            pl.BlockSpec((1, TM, d), rowblk),
            pl.BlockSpec((1, TM, d), rowblk),
            pl.BlockSpec(wconv.shape, const),
            pl.BlockSpec((1, d), const),
            wspec(wa), wspec(wc), wspec(wo), wspec(wr_t),
        ],
        out_specs=[
            pl.BlockSpec((1, TM, d), rowblk),
            pl.BlockSpec((1, TM, d), rowblk),
            pl.BlockSpec((1, n_exp, TM), lambda bi, j: (bi, 0, j)),
        ],
        out_shape=[
            jax.ShapeDtypeStruct((b, li, d), F32),
            jax.ShapeDtypeStruct((b, li, d), BF16),
            jax.ShapeDtypeStruct((b, n_exp, li), F32),
        ],
        compiler_params=_cparams(2),
        name="mix_stage",
    )(x, meta_blk, attn, cb, sprod, sprod, sprod, ga, gc, wconv, gffn, wa, wc, wo, wr_t)


def _expert_kernel(xs_ref, wg_ref, wu_ref, wd_ref, y_ref):
    x = xs_ref[0]
    g = _dot(x, wg_ref[0])
    u = _dot(x, wu_ref[0])
    he = (g * jax.nn.sigmoid(g) * u).astype(BF16)
    y_ref[0] = _dot(he, wd_ref[0]).astype(y_ref.dtype)


def _stage_d(xs, wg, wu, wd):
    e, cap_p, d = xs.shape
    de = wg.shape[-1]
    tile = lambda ei, t: (ei, t, 0)
    wmap = lambda ei, t: (ei, 0, 0)
    return pl.pallas_call(
        _expert_kernel,
        grid=(e, cap_p // TE),
        in_specs=[
            pl.BlockSpec((1, TE, d), tile),
            pl.BlockSpec((1, d, de), wmap),
            pl.BlockSpec((1, d, de), wmap),
            pl.BlockSpec((1, de, d), wmap),
        ],
        out_specs=pl.BlockSpec((1, TE, d), tile),
        out_shape=jax.ShapeDtypeStruct((e, cap_p, d), BF16),
        compiler_params=_cparams(2),
        name="expert_stage",
    )(xs, wg, wu, wd)


def _combine_kernel(start_ref, rounds_ref, x1_ref, slot_ref, gate_ref, ye_ref, o_ref, ybuf, sem, *,
                    n_exp, nsteps):
    lin = pl.program_id(0) * pl.num_programs(1) + pl.program_id(1)

    def chunk_start(step, e, rnd):
        return jnp.minimum(start_ref[step * n_exp + e] + rnd * CH, ye_ref.shape[1] - CH)

    def chunk_copy(step, e, rnd, buf):
        st = pl.multiple_of(chunk_start(step, e, rnd), HALO)
        return pltpu.make_async_copy(ye_ref.at[e, pl.ds(st, CH), :],
                                     ybuf.at[buf, pl.ds(e * CH, CH), :], sem.at[buf])

    def fetch(step, rnd, buf):
        for e in range(n_exp):
            chunk_copy(step, e, rnd, buf).start()

    def wait(step, rnd, buf):
        for e in range(n_exp):
            chunk_copy(step, e, rnd, buf).wait()

    buf = lin % 2

    @pl.when(lin == 0)
    def _():
        fetch(0, 0, 0)

    @pl.when(lin + 1 < nsteps)
    def _():
        fetch(lin + 1, 0, 1 - buf)

    slot = jnp.transpose(slot_ref[0])
    gate = jnp.transpose(gate_ref[0])
    lane_c = lax.broadcasted_iota(jnp.int32, (1, CH), 1)

    def gathered(rnd):
        perm = []
        for e in range(n_exp):
            st = chunk_start(lin, e, rnd)
            first_lane = start_ref[lin * n_exp + e] + rnd * CH - st
            lanes = jnp.where(lane_c >= first_lane, lane_c, CH + 1).astype(F32).astype(BF16)
            offset = jnp.clip(slot[:, e:e + 1] - st, -1, CH).astype(F32).astype(BF16)
            perm.append(jnp.where(offset == lanes, gate[:, e:e + 1].astype(BF16), jnp.zeros((), BF16)))
        acc = None
        for k in range(0, n_exp, 2):
            part = _dot(jnp.concatenate(perm[k:k + 2], axis=1), ybuf[buf, k * CH:(k + 2) * CH, :])
            acc = part if acc is None else acc + part
        return acc

    wait(lin, 0, buf)
    o_ref[0] = x1_ref[0] + gathered(0)

    def extra(rnd, carry):
        fetch(lin, rnd, buf)
        wait(lin, rnd, buf)
        o_ref[0] += gathered(rnd)
        return carry

    lax.fori_loop(1, rounds_ref[lin], extra, 0)


def _stage_e(x1, slot, gate, ye, start_al, rounds, s):
    b, li, d = x1.shape
    n_exp = slot.shape[1]
    assert CH + 1 <= BYTE
    ntile = s // TM
    tile = lambda bi, j, *_: (bi, j, 0)
    return pl.pallas_call(
        functools.partial(_combine_kernel, n_exp=n_exp, nsteps=b * ntile),
        grid_spec=pltpu.PrefetchScalarGridSpec(
            num_scalar_prefetch=2,
            grid=(b, ntile),
            in_specs=[
                pl.BlockSpec((1, TM, d), tile),
                pl.BlockSpec((1, n_exp, TM), lambda bi, j, *_: (bi, 0, j)),
                pl.BlockSpec((1, n_exp, TM), lambda bi, j, *_: (bi, 0, j)),
                pl.BlockSpec(memory_space=pl.ANY),
            ],
            out_specs=pl.BlockSpec((1, TM, d), tile),
            scratch_shapes=[pltpu.VMEM((2, n_exp * CH, d), BF16), pltpu.SemaphoreType.DMA((2,))],
        ),
        out_shape=jax.ShapeDtypeStruct((b, s, d), F32),
        compiler_params=_cparams(2),
        name="combine_stage",
    )(start_al, rounds, x1, slot, gate, ye)


def _q_head_perm():
    groups = N_HEADS // N_KV_HEADS
    cols = []
    for m in range(N_KV_HEADS // 2):
        for i in range(groups):
            for h in (2 * groups * m + i, 2 * groups * m + groups + i):
                cols.extend(range(h * HEAD_DIM, (h + 1) * HEAD_DIM))
    return np.asarray(cols, np.int32)


def _rope_tables(s):
    pos = jnp.concatenate([N_META + jnp.arange(s, dtype=F32), jnp.zeros((META_ROW0,), F32),
                           jnp.arange(N_META, dtype=F32)])
    inv = ROPE_THETA ** (-jnp.arange(0, HEAD_DIM, 2, dtype=F32) / HEAD_DIM)
    ang = pos[:, None] * inv[None, :]
    reps = LANES // (HEAD_DIM // 2)
    cos2 = jnp.tile(jnp.cos(ang), (1, reps))
    sgn = jnp.where((jnp.arange(LANES) % HEAD_DIM) < HEAD_DIM // 2, -1.0, 1.0).astype(F32)
    sin2 = jnp.tile(jnp.sin(ang), (1, reps)) * sgn[None, :]
    return cos2, sin2


def _sink_rows(sink):
    groups = N_HEADS // N_KV_HEADS
    heads = np.asarray([[2 * groups * m + groups * half + i
                         for i in range(groups) for half in range(2)]
                        for m in range(N_KV_HEADS // 2)], np.int32)
    rows = jnp.repeat(sink.astype(F32)[heads] * LOG2E, BLK, axis=1)
    return rows[:, None, :]


def _threshold_kernel(aff_ref, thr_ref, *, cap):
    bits = pltpu.bitcast(aff_ref[...], jnp.int32)

    def step(i, prefix):
        cand = prefix | jnp.left_shift(jnp.int32(1), 30 - i)
        count = jnp.sum((bits >= cand[None]).astype(jnp.int32), axis=2, keepdims=True)
        return jnp.where(jnp.sum(count, axis=0) >= cap, cand, prefix)

    prefix = lax.fori_loop(0, 31, step, jnp.zeros((bits.shape[1], 1), jnp.int32))
    thr_ref[...] = jnp.broadcast_to(pltpu.bitcast(prefix, F32), thr_ref.shape)


def _thresholds(aff_t, cap):
    n_exp = aff_t.shape[1]
    out = pl.pallas_call(
        functools.partial(_threshold_kernel, cap=cap),
        out_shape=jax.ShapeDtypeStruct((n_exp, LANES), F32),
        compiler_params=pltpu.CompilerParams(vmem_limit_bytes=VMEM_LIMIT),
        name="threshold_stage",
    )(aff_t)
    return out[:, 0]


def _prefix_counts(mask):
    b, e, li = mask.shape
    g = li // LANES
    tri_l = jnp.asarray(np.tri(LANES, dtype=np.float32).T, BF16)
    tri_g = jnp.asarray(np.tri(b * g, k=-1, dtype=np.float32).T, BF16)
    local = jnp.einsum("begl,lm->begm", mask.reshape(b, e, g, LANES).astype(BF16), tri_l,
                       preferred_element_type=F32)
    totals = jnp.transpose(local[:, :, :, LANES - 1], (1, 0, 2)).reshape(e, b * g)
    before = jnp.einsum("eg,gh->eh", totals.astype(BF16), tri_g, preferred_element_type=F32)
    before = jnp.transpose(before.reshape(e, b, g), (1, 0, 2))
    return local.astype(jnp.int32), before.astype(jnp.int32)


def _route(aff, s):
    b, n_exp, li = aff.shape
    grp_seq = li // LANES
    ngrp = b * grp_seq
    n_tok = b * (s + N_META)
    cap = max(1, CAPACITY_FACTOR * n_tok // n_exp)
    cap_p = max(-(-cap // TE) * TE, CH)
    thr = _thresholds(aff, cap)[None, :, None]
    above = aff > thr
    equal = aff == thr
    room = cap - jnp.sum(above, axis=(0, 2), dtype=jnp.int32)[None, :, None]

    def earliest_ties(_):
        local, before = _prefix_counts(equal)
        eq_upto = (local + before[:, :, :, None]).reshape(b, n_exp, li)
        eq_real_end = eq_upto[:, :, s - 1:s]
        eq_before_seq = eq_upto[:, :, 0:1] - equal[:, :, 0:1]
        eq_real = eq_real_end - eq_before_seq
        eq_meta = eq_upto[:, :, li - 1:li] - eq_real_end
        is_real = (jnp.arange(li) < s)[None, None, :]
        rank = eq_upto + jnp.where(is_real, eq_meta, -eq_real)
        return above | (equal & (rank <= room))

    all_ties_fit = jnp.all(jnp.sum(equal, axis=(0, 2), dtype=jnp.int32)[None, :, None] == room)
    sel = lax.cond(all_ties_fit, lambda _: above | equal, earliest_ties, None)

    local, grp_before = _prefix_counts(sel)
    upto = (local + grp_before[:, :, :, None]).reshape(b, n_exp, li)
    slot = jnp.where(sel, upto - 1, UNSELECTED)
    gate = jnp.where(sel, aff, 0.0)

    tile_grp = TM // LANES
    start = grp_before[:, :, 0:s // LANES:tile_grp]
    end = grp_before[:, :, tile_grp:s // LANES + 1:tile_grp]
    start_al = start // HALO * HALO
    rounds = jnp.maximum(jnp.max(-(-(end - start_al) // CH), axis=1), 1)
    start_al = jnp.transpose(start_al, (0, 2, 1)).reshape(-1)

    local = jnp.transpose(local, (1, 0, 2, 3)).reshape(n_exp, ngrp, LANES)
    grp_before = jnp.transpose(grp_before, (1, 0, 2)).reshape(n_exp, ngrp)
    grp_end = jnp.concatenate([grp_before[:, 1:], jnp.full((n_exp, 1), cap, jnp.int32)], axis=1)
    gidx = jnp.arange(ngrp, dtype=jnp.int32)[None, :]
    byte = lambda v: jnp.stack([v // BYTE, v % BYTE], axis=-1)
    table = jnp.concatenate([local, byte(grp_before), jnp.broadcast_to(byte(gidx), (n_exp, ngrp, 2))],
                            axis=-1).astype(BF16)
    slots = jnp.arange(cap_p, dtype=jnp.int32)[None, :, None]
    in_grp = ((grp_before[:, None, :] <= slots) & (slots < grp_end[:, None, :])).astype(BF16)
    hit = jnp.einsum("esg,egc->esc", in_grp, table, preferred_element_type=F32).astype(jnp.int32)
    j = slots[:, :, 0] - (hit[:, :, LANES] * BYTE + hit[:, :, LANES + 1])
    lane = jnp.sum(hit[:, :, :LANES] <= j[:, :, None], axis=-1, dtype=jnp.int32)
    rows = (hit[:, :, LANES + 2] * BYTE + hit[:, :, LANES + 3]) * LANES + lane
    rows = jnp.where(slots[:, :, 0] < cap, rows, 0)
    return rows, slot, gate, start_al, rounds.reshape(-1)


def _project(x, meta_blk, prm, cast_srcs=()):
    b, s, d = x.shape
    assert s % TM == 0 and d % LANES == 0
    cos2, sin2 = _rope_tables(s)
    return _stage_a(x, meta_blk, cos2, sin2, prm["gmix"], prm["gq2"], prm["gk2"], prm["bd"],
                    prm["w_in"], cast_srcs)


def _encoder(x, meta_blk, prm, projected, expert_w):
    b, s, d = x.shape
    li = s + BLK
    q, k, v, cb, sprod, ga, gc = projected
    attn = _stage_b(q, k, v, prm["sink_rows"])
    x1, h2, aff_t = _stage_c(x, meta_blk, attn, cb, sprod, ga, gc, prm["wconv"], prm["gffn"],
                             prm["wa"], prm["wc"], prm["wo"], prm["wr_t"])
    rows, slot, gate, start_al, rounds = _route(aff_t, s)
    xs = h2.reshape(b * li, d)[rows]
    ye = _stage_d(xs, *expert_w)
    return _stage_e(x1, slot, gate, ye, start_al, rounds, s)


def kernel(x_prompt, x_sample, meta_tokens, g_mix, w_in, g_q, g_k, sink_logits, w_conv, w_attn_out,
           w_conv_out, w_out, g_ffn, w_router, w_expert_gate, w_expert_up, w_expert_down):
    assert w_in.shape[0] == 1, "single-layer block"
    d = x_prompt.shape[-1]
    qw = N_HEADS * HEAD_DIM
    perm = _q_head_perm()
    w_in0 = w_in[0]
    lane = np.arange(LANES)
    prm = {
        "w_in": jnp.concatenate([w_in0[:, :qw][:, perm], w_in0[:, qw:]], axis=1).astype(BF16),
        "gmix": g_mix[0][None, :].astype(F32),
        "gq2": jnp.tile(g_q[0], LANES // HEAD_DIM)[None, :].astype(F32),
        "gk2": jnp.tile(g_k[0], LANES // HEAD_DIM)[None, :].astype(F32),
        "bd": jnp.asarray((lane[:, None] // HEAD_DIM) == (lane[None, :] // HEAD_DIM), BF16),
        "sink_rows": _sink_rows(sink_logits[0]),
        "wconv": w_conv[0].astype(F32),
        "gffn": g_ffn[0][None, :].astype(F32),
        "wa": w_attn_out[0][perm, :].astype(BF16),
        "wc": w_conv_out[0].astype(BF16),
        "wo": w_out[0].astype(BF16),
        "wr_t": jnp.transpose(w_router[0]).astype(BF16),
    }
    meta_blk = jnp.concatenate([jnp.zeros((META_ROW0, d), F32), meta_tokens.astype(F32)], axis=0)
    expert_w = [w_expert_gate[0], w_expert_up[0], w_expert_down[0]]
    flat = [w.astype(F32).reshape(-1, w.shape[-1]) for w in expert_w]
    steps = x_prompt.shape[0] * (x_prompt.shape[1] // TM)
    if all(_cast_rows(w, steps) for w in flat):
        proj_prompt, cast = _project(x_prompt, meta_blk, prm, flat)
        expert_w = [c.reshape(w.shape) for c, w in zip(cast, expert_w)]
    else:
        proj_prompt, _ = _project(x_prompt, meta_blk, prm)
        expert_w = [w.astype(BF16) for w in expert_w]
    proj_sample, _ = _project(x_sample, meta_blk, prm)
    return (_encoder(x_prompt, meta_blk, prm, proj_prompt, expert_w),
            _encoder(x_sample, meta_blk, prm, proj_sample, expert_w))
```
